```python
import jax, jax.numpy as jnp
from jax import lax
import numpy as np

D_MODEL = 1024
BATCH = 8
SEQ = 2048
DEPTH = 1

RWKV_HEADS = 8
RWKV_HEAD_DIM = 64
RWKV_WIDTH = RWKV_HEADS * RWKV_HEAD_DIM
DECAY_LORA = 64
ICLR_LORA = 64
GATE_LORA = 128
GN_EPS = 64e-5
N_DIR = 2
GMLP_GROUPS = 8
GMLP_WIDTH = 512
GMLP_GROUP_DIM = GMLP_WIDTH // GMLP_GROUPS
CHUNK = 128
N_BRANCH = 2
MEM_LEN = 256
XATTN_HEADS = 4
XATTN_HEAD_DIM = D_MODEL // XATTN_HEADS
PEER_HEADS = 8
N_KEYS = 128
N_EXPERTS = N_KEYS * N_KEYS
PEER_QDIM = 256
PEER_HALF = PEER_QDIM // 2
PEER_TOPK = 16
PEER_TOKEN_BLOCK = 128
LN_EPS = 1e-5
ALPHA = (2.0 * DEPTH) ** 0.25
BETA = (8.0 * DEPTH) ** -0.25

RWKV_SPLITS = [int(c) for c in np.cumsum([RWKV_WIDTH, RWKV_WIDTH, RWKV_WIDTH,
                                          N_DIR * DECAY_LORA, N_DIR * ICLR_LORA])]
RWKV_COLS = 3 * RWKV_WIDTH + N_DIR * DECAY_LORA + N_DIR * ICLR_LORA + GATE_LORA
GMLP_COLS = 2 * GMLP_WIDTH
GATE_COLS = N_BRANCH * D_MODEL
IN_COLS = RWKV_COLS + GMLP_COLS + GATE_COLS

kernel_name = "hybrid_rwkv7_gmlp_peer_deepnorm_encoder"


def layer_norm(x, g, b, eps=LN_EPS):
    xf = x.astype(jnp.float32)
    mu = jnp.mean(xf, -1, keepdims=True)
    var = jnp.mean(jnp.square(xf - mu), -1, keepdims=True)
    return ((xf - mu) * lax.rsqrt(var + eps) * g.astype(jnp.float32) + b.astype(jnp.float32)).astype(x.dtype)


def centred_shift(p):
    zero = jnp.zeros_like(p[:, :1])
    prev = jnp.concatenate([zero, p[:, :-1]], axis=1)
    nxt = jnp.concatenate([p[:, 1:], zero], axis=1)
    return 0.5 * (prev + nxt)


def dir_time_major(fwd, bwd):
    t = jnp.stack([fwd, jnp.flip(bwd, axis=1)], axis=0)
    return jnp.moveaxis(t, 2, 0)


def rwkv7_scan(r, w, k, v, a, b):
    n_dir, bsz, nh, n = r.shape[1:]
    s0 = jnp.zeros((n_dir, bsz, nh, n, n), jnp.float32)

    def step(s, inp):
        r_t, w_t, k_t, v_t, a_t, b_t = inp
        sa = jnp.einsum('zbhij,zbhj->zbhi', s, a_t)
        s = s * w_t[..., None, :] + sa[..., :, None] * b_t[..., None, :] + v_t[..., :, None] * k_t[..., None, :]
        y = jnp.einsum('zbhij,zbhj->zbhi', s, r_t)
        return s, y

    _, y = lax.scan(step, s0, (r, w, k, v, a, b))
    return y


def rwkv7_branch(p, mu, w0, w2, a0, a2, g2, k_k, k_a, r_k, gn_g, gn_b):
    bsz, s, _ = p.shape
    f32 = jnp.float32
    p = p + mu * (centred_shift(p) - p)
    r, k, v, wd, ad, gd = jnp.split(p.astype(f32), RWKV_SPLITS, axis=-1)
    wd = wd.reshape(bsz, s, N_DIR, DECAY_LORA)
    ad = ad.reshape(bsz, s, N_DIR, ICLR_LORA)
    w_log = -jax.nn.softplus(-(w0.astype(f32) + jnp.einsum('bszr,zrc->bszc', jnp.tanh(wd), w2.astype(f32)))) - 0.5
    decay = jnp.exp(-jnp.exp(w_log))
    a = jax.nn.sigmoid(a0.astype(f32) + jnp.einsum('bszr,zrc->bszc', ad, a2.astype(f32)))
    g = jnp.einsum('bsr,rc->bsc', jax.nn.sigmoid(gd), g2.astype(f32))
    kk = (k * k_k.astype(f32)).reshape(bsz, s, RWKV_HEADS, RWKV_HEAD_DIM)
    kk = kk / jnp.maximum(jnp.linalg.norm(kk, axis=-1, keepdims=True), 1e-12)
    kk = kk.reshape(bsz, s, RWKV_WIDTH)
    k_dir = k[:, :, None, :] * (1.0 + (a - 1.0) * k_a.astype(f32))
    b_vec = kk[:, :, None, :] * a

    hs = lambda t: t.reshape(bsz, s, RWKV_HEADS, RWKV_HEAD_DIM)
    shared = lambda t: dir_time_major(hs(t), hs(t))
    per_dir = lambda t: dir_time_major(hs(t[:, :, 0]), hs(t[:, :, 1]))
    y = rwkv7_scan(shared(r), per_dir(decay), per_dir(k_dir), shared(v), shared(-kk), per_dir(b_vec))
    y = y[:, 0] + jnp.flip(y[:, 1], axis=0)
    y = jnp.moveaxis(y, 0, 1)
    mu_y = jnp.mean(y, -1, keepdims=True)
    var_y = jnp.mean(jnp.square(y - mu_y), -1, keepdims=True)
    y = ((y - mu_y) * lax.rsqrt(var_y + GN_EPS)).reshape(bsz, s, RWKV_WIDTH) * gn_g.astype(f32) + gn_b.astype(f32)
    rk = jnp.sum((r[:, :, None, :] * k_dir).reshape(bsz, s, N_DIR, RWKV_HEADS, RWKV_HEAD_DIM) * r_k.astype(f32), axis=(2, 4))
    y = y + (rk[..., None] * hs(v)).reshape(bsz, s, RWKV_WIDTH)
    return (y * g).astype(p.dtype)


def gmlp_branch(p, ln_g, ln_b, w_s, b_s):
    bsz, s, _ = p.shape
    u, v = jnp.split(jax.nn.gelu(p, approximate=False), 2, axis=-1)
    v = layer_norm(v, ln_g, ln_b)
    v = v.reshape(bsz, s // CHUNK, CHUNK, GMLP_GROUPS, GMLP_GROUP_DIM)
    sv = jnp.einsum('gpq,bcqgd->bcpgd', w_s, v) + b_s.T[None, None, :, :, None]
    return u * sv.reshape(bsz, s, GMLP_WIDTH)


def hybrid_mixer(h, w_in, rwkv_mu, rwkv_w0, rwkv_w2, rwkv_a0, rwkv_a2, rwkv_g2, rwkv_k_k, rwkv_k_a,
                 rwkv_r_k, rwkv_gn_g, rwkv_gn_b, gmlp_ln_g, gmlp_ln_b, gmlp_w_s, gmlp_b_s, w_branch, w_mix_out):
    bsz, s, d = h.shape
    p = jnp.einsum('bsd,dc->bsc', h, w_in)
    p_a, p_b, p_g = jnp.split(p, [RWKV_COLS, RWKV_COLS + GMLP_COLS], axis=-1)
    y_a = rwkv7_branch(p_a, rwkv_mu, rwkv_w0, rwkv_w2, rwkv_a0, rwkv_a2, rwkv_g2, rwkv_k_k, rwkv_k_a,
                       rwkv_r_k, rwkv_gn_g, rwkv_gn_b)
    y_b = gmlp_branch(p_b, gmlp_ln_g, gmlp_ln_b, gmlp_w_s, gmlp_b_s)
    y = jnp.stack([y_a, y_b], axis=2)
    branch = jnp.einsum('bsnc,ncd->bsnd', y, w_branch)
    gates = jax.nn.sigmoid(p_g.reshape(bsz, s, N_BRANCH, d))
    merged = jnp.sum(gates * branch, axis=2)
    return jnp.einsum('bsd,de->bse', merged, w_mix_out)


def memory_xattn(h, mem_n, w_q, w_kv, w_o):
    bsz, s, d = h.shape
    m = mem_n.shape[1]
    q = jnp.einsum('bsd,de->bse', h, w_q).reshape(bsz, s, XATTN_HEADS, XATTN_HEAD_DIM)
    k, v = jnp.split(jnp.einsum('bmd,de->bme', mem_n, w_kv), 2, axis=-1)
    k = k.reshape(bsz, m, XATTN_HEADS, XATTN_HEAD_DIM)
    v = v.reshape(bsz, m, XATTN_HEADS, XATTN_HEAD_DIM)
    scores = jnp.einsum('bshd,bmhd->bhsm', q, k).astype(jnp.float32) * (XATTN_HEAD_DIM ** -0.5)
    prob = jax.nn.softmax(scores, axis=-1).astype(h.dtype)
    o = jnp.einsum('bhsm,bmhd->bshd', prob, v).reshape(bsz, s, d)
    return jnp.einsum('bsd,de->bse', o, w_o)


def peer(h, w_query, sub_keys, u_table, v_table):
    bsz, s, d = h.shape
    q = jnp.einsum('bsd,dq->bsq', h, w_query).reshape(bsz, s, PEER_HEADS, 2, PEER_HALF)
    sc = jnp.einsum('bshzc,zkc->bshzk', q, sub_keys).astype(jnp.float32)
    top_s, top_i = lax.top_k(sc, PEER_TOPK)
    cand_s = top_s[..., 0, :, None] + top_s[..., 1, None, :]
    cand_i = top_i[..., 0, :, None] * N_KEYS + top_i[..., 1, None, :]
    cand_s = cand_s.reshape(bsz, s, PEER_HEADS, PEER_TOPK * PEER_TOPK)
    cand_i = cand_i.reshape(bsz, s, PEER_HEADS, PEER_TOPK * PEER_TOPK)
    best_s, best_pos = lax.top_k(cand_s, PEER_TOPK)
    expert = jnp.take_along_axis(cand_i, best_pos, axis=-1)
    gate = jax.nn.softmax(best_s, axis=-1).astype(h.dtype)
    n_blk = (bsz * s) // PEER_TOKEN_BLOCK
    xb = h.reshape(n_blk, PEER_TOKEN_BLOCK, d)
    eb = expert.reshape(n_blk, PEER_TOKEN_BLOCK, PEER_HEADS, PEER_TOPK)
    gb = gate.reshape(n_blk, PEER_TOKEN_BLOCK, PEER_HEADS, PEER_TOPK)

    def token_block(args):
        xt, et, gt = args
        u = jnp.take(u_table, et, axis=0)
        act = jax.nn.gelu(jnp.einsum('thkd,td->thk', u, xt), approximate=False)
        vv = jnp.take(v_table, et, axis=0)
        return jnp.einsum('thk,thkd->td', gt * act, vv)

    y = lax.map(token_block, (xb, eb, gb))
    return y.reshape(bsz, s, d)


def setup_inputs(seed: int = 0) -> dict:
    key = jax.random.key(seed)
    keys = list(jax.random.split(key, 64))

    def nk():
        return keys.pop()

    def nrm(shape, scale):
        return jax.random.normal(nk(), shape, jnp.float32) * scale

    def gain(shape):
        return 1.0 + nrm(shape, 0.02)

    L, D = DEPTH, D_MODEL
    col_scale = jnp.ones((IN_COLS,), jnp.float32).at[2 * RWKV_WIDTH:3 * RWKV_WIDTH].set(BETA)
    kv_scale = jnp.ones((2 * D,), jnp.float32).at[D:].set(BETA)
    return {
        "x": nrm((BATCH, SEQ, D), 1.0),
        "mem": nrm((BATCH, MEM_LEN, D), 1.0),
        "ln_emb_g": gain((D,)),
        "ln_emb_b": nrm((D,), 0.02),
        "w_in": nrm((L, D, IN_COLS), D ** -0.5) * col_scale,
        "rwkv_mu": jax.random.uniform(nk(), (L, RWKV_COLS), jnp.float32, 0.0, 1.0),
        "rwkv_w0": jax.random.uniform(nk(), (L, N_DIR, RWKV_WIDTH), jnp.float32, -6.0, -1.0),
        "rwkv_w2": nrm((L, N_DIR, DECAY_LORA, RWKV_WIDTH), 0.1),
        "rwkv_a0": nrm((L, N_DIR, RWKV_WIDTH), 0.1),
        "rwkv_a2": nrm((L, N_DIR, ICLR_LORA, RWKV_WIDTH), 0.1),
        "rwkv_g2": nrm((L, GATE_LORA, RWKV_WIDTH), GATE_LORA ** -0.5),
        "rwkv_k_k": 0.85 + nrm((L, RWKV_WIDTH), 0.02),
        "rwkv_k_a": gain((L, RWKV_WIDTH)),
        "rwkv_r_k": nrm((L, RWKV_HEADS, RWKV_HEAD_DIM), 0.1),
        "rwkv_gn_g": gain((L, RWKV_WIDTH)),
        "rwkv_gn_b": nrm((L, RWKV_WIDTH), 0.02),
        "gmlp_ln_g": gain((L, GMLP_WIDTH)),
        "gmlp_ln_b": nrm((L, GMLP_WIDTH), 0.02),
        "gmlp_w_s": nrm((L, GMLP_GROUPS, CHUNK, CHUNK), CHUNK ** -0.5),
        "gmlp_b_s": gain((L, GMLP_GROUPS, CHUNK)),
        "w_branch": nrm((L, N_BRANCH, RWKV_WIDTH, D), RWKV_WIDTH ** -0.5),
        "w_mix_out": nrm((L, D, D), D ** -0.5) * BETA,
        "ln1_g": gain((L, D)),
        "ln1_b": nrm((L, D), 0.02),
        "mem_ln_g": gain((L, D)),
        "mem_ln_b": nrm((L, D), 0.02),
        "xattn_w_q": nrm((L, D, D), D ** -0.5),
        "xattn_w_kv": nrm((L, D, 2 * D), D ** -0.5) * kv_scale,
        "xattn_w_o": nrm((L, D, D), D ** -0.5) * BETA,
        "ln2_g": gain((L, D)),
        "ln2_b": nrm((L, D), 0.02),
        "peer_w_query": nrm((L, D, PEER_HEADS * PEER_QDIM), D ** -0.5),
        "peer_sub_keys": nrm((L, 2, N_KEYS, PEER_HALF), PEER_HALF ** -0.5),
        "peer_u": nrm((L, N_EXPERTS, D), D ** -0.5),
        "peer_v": nrm((L, N_EXPERTS, D), BETA * PEER_HEADS ** -0.5),
        "ln3_g": gain((L, D)),
        "ln3_b": nrm((L, D), 0.02),
    }


def reference(x, mem, ln_emb_g, ln_emb_b, w_in, rwkv_mu, rwkv_w0, rwkv_w2, rwkv_a0, rwkv_a2, rwkv_g2,
              rwkv_k_k, rwkv_k_a, rwkv_r_k, rwkv_gn_g, rwkv_gn_b, gmlp_ln_g, gmlp_ln_b, gmlp_w_s, gmlp_b_s,
              w_branch, w_mix_out, ln1_g, ln1_b, mem_ln_g, mem_ln_b, xattn_w_q, xattn_w_kv, xattn_w_o,
              ln2_g, ln2_b, peer_w_query, peer_sub_keys, peer_u, peer_v, ln3_g, ln3_b):
    h = layer_norm(x, ln_emb_g, ln_emb_b)
    for l in range(DEPTH):
        mix = hybrid_mixer(h, w_in[l], rwkv_mu[l], rwkv_w0[l], rwkv_w2[l], rwkv_a0[l], rwkv_a2[l], rwkv_g2[l],
                           rwkv_k_k[l], rwkv_k_a[l], rwkv_r_k[l], rwkv_gn_g[l], rwkv_gn_b[l],
                           gmlp_ln_g[l], gmlp_ln_b[l], gmlp_w_s[l], gmlp_b_s[l], w_branch[l], w_mix_out[l])
        h = layer_norm(ALPHA * h + mix, ln1_g[l], ln1_b[l])
        mem_n = layer_norm(mem, mem_ln_g[l], mem_ln_b[l])
        h = layer_norm(ALPHA * h + memory_xattn(h, mem_n, xattn_w_q[l], xattn_w_kv[l], xattn_w_o[l]),
                       ln2_g[l], ln2_b[l])
        h = layer_norm(ALPHA * h + peer(h, peer_w_query[l], peer_sub_keys[l], peer_u[l], peer_v[l]),
                       ln3_g[l], ln3_b[l])
    return h
```

```python
import functools

import numpy as np
import jax
import jax.numpy as jnp
from jax import lax
from jax.experimental import pallas as pl
from jax.experimental.pallas import tpu as pltpu

F32 = jnp.float32
BF16 = jnp.bfloat16
HIGHEST = lax.Precision.HIGHEST

LN_EPS = 1e-5
GN_EPS = 64e-5
LANES = 128
HEAD_DIM = 64
SCAN_CHUNK = 64
TOPK = 16
N_KEYS = 128
VMEM_LIMIT = 56 * 1024 * 1024


def _cparams(*sem):
    return pltpu.CompilerParams(dimension_semantics=sem, vmem_limit_bytes=VMEM_LIMIT)


def _ln(x, g, b, eps=LN_EPS):
    mu = jnp.mean(x, axis=-1, keepdims=True)
    xc = x - mu
    var = jnp.mean(xc * xc, axis=-1, keepdims=True)
    return xc * lax.rsqrt(var + eps) * g + b


def _mm(a, b):
    return jnp.dot(a.astype(BF16), b.astype(BF16), preferred_element_type=F32)


def _mm_hi(a, b):
    return jnp.dot(a, b, precision=HIGHEST, preferred_element_type=F32)


def _mm_nt_hi(a, b):
    return lax.dot_general(a, b, (((1,), (1,)), ((), ())), precision=HIGHEST,
                           preferred_element_type=F32)


def _mm_tn_hi(a, b):
    return lax.dot_general(a, b, (((0,), (0,)), ((), ())), precision=HIGHEST,
                           preferred_element_type=F32)


def _gelu(x):
    return 0.5 * x * (1.0 + lax.erf(x * np.float32(1.0 / np.sqrt(2.0))))


def _sigmoid(x):
    return 1.0 / (1.0 + jnp.exp(-x))


def _full(shape):
    nd = len(shape)
    return pl.BlockSpec(shape, lambda *_: (0,) * nd)


def _inproj_kernel(x_ref, g_ref, b_ref, wa_ref, wb_ref, wg_ref, pa_ref, pb_ref, pg_ref):
    h = _ln(x_ref[...], g_ref[...], b_ref[...]).astype(BF16)
    pa_ref[...] = jnp.dot(h, wa_ref[...], preferred_element_type=F32)
    pb_ref[...] = jnp.dot(h, wb_ref[...], preferred_element_type=F32)
    pg_ref[...] = jnp.dot(h, wg_ref[...], preferred_element_type=F32)


def _inproj(x2, g, b, wa, wb, wg, tm):
    n, d = x2.shape
    ca, cb, cg = wa.shape[1], wb.shape[1], wg.shape[1]
    row = lambda c: pl.BlockSpec((tm, c), lambda i: (i, 0))
    return pl.pallas_call(
        _inproj_kernel,
        grid=(n // tm,),
        in_specs=[row(d), _full((1, d)), _full((1, d)), _full((d, ca)), _full((d, cb)), _full((d, cg))],
        out_specs=[row(ca), row(cb), row(cg)],
        out_shape=[jax.ShapeDtypeStruct((n, c), F32) for c in (ca, cb, cg)],
        compiler_params=_cparams("parallel"),
        name="inproj",
    )(x2, g, b, wa, wb, wg)


def _rwkv_pre_kernel(tiles_per_seq, width,
                     p_ref, prev_ref, next_ref, mu_ref, w0_ref, w2_ref, a0_ref, a2_ref, g2_ref,
                     kk_ref, ka_ref, rk_ref, bd_ref,
                     r_out, v_out, na_out, lw_out, kd_out, bv_out, g_out, bonus_out):
    i = pl.program_id(0)
    pos = i % tiles_per_seq
    p = p_ref[...]
    tm = p.shape[0]
    row = lax.broadcasted_iota(jnp.int32, (tm, 1), 0)
    prev_row = jnp.where(pos == 0, 0.0, prev_ref[7:8, :])
    next_row = jnp.where(pos == tiles_per_seq - 1, 0.0, next_ref[0:1, :])
    prev = jnp.where(row == 0, prev_row, pltpu.roll(p, 1, 0))
    nxt = jnp.where(row == tm - 1, next_row, pltpu.roll(p, tm - 1, 0))
    ps = p + mu_ref[...] * (0.5 * (prev + nxt) - p)

    w = width
    r = ps[:, 0:w]
    k = ps[:, w:2 * w]
    v = ps[:, 2 * w:3 * w]
    wd = ps[:, 3 * w:3 * w + LANES]
    ad = ps[:, 3 * w + LANES:3 * w + 2 * LANES]
    gd = ps[:, 3 * w + 2 * LANES:3 * w + 3 * LANES]

    wz = w0_ref[...] + _mm_hi(jnp.tanh(wd), w2_ref[...])
    w_log = -(jnp.maximum(-wz, 0.0) + jnp.log(1.0 + jnp.exp(-jnp.abs(wz)))) - 0.5
    log_decay = -jnp.exp(w_log)
    a = _sigmoid(a0_ref[...] + _mm_hi(ad, a2_ref[...]))
    g = _mm_hi(_sigmoid(gd), g2_ref[...])

    bd = bd_ref[...]
    kk = k * kk_ref[...]
    nrm = jnp.sqrt(_mm_hi(kk * kk, bd))
    kk = kk / jnp.maximum(nrm, 1e-12)

    ka = ka_ref[...]
    k0 = k * (1.0 + (a[:, 0:w] - 1.0) * ka)
    k1 = k * (1.0 + (a[:, w:2 * w] - 1.0) * ka)
    rk = _mm_hi(r * (k0 + k1) * rk_ref[...], bd)

    r_out[...] = r
    v_out[...] = v
    na_out[...] = -kk
    lw_out[0] = log_decay[:, 0:w]
    lw_out[1] = log_decay[:, w:2 * w]
    kd_out[0] = k0
    kd_out[1] = k1
    bv_out[0] = kk * a[:, 0:w]
    bv_out[1] = kk * a[:, w:2 * w]
    g_out[...] = g
    bonus_out[...] = rk * v


def _rwkv_pre(pa, seq, mu, w0, w2bd, a0, a2bd, g2, k_k, k_a, r_k, bd, tm):
    n, ca = pa.shape
    w = bd.shape[0]
    tps = seq // tm
    hb = tm // 8
    nb8 = n // 8
    row = lambda c: pl.BlockSpec((tm, c), lambda i: (i, 0))
    row2 = pl.BlockSpec((2, tm, w), lambda i: (0, i, 0))
    outs = [jax.ShapeDtypeStruct((n, w), F32)] * 3 + [jax.ShapeDtypeStruct((2, n, w), F32)] * 3 + \
           [jax.ShapeDtypeStruct((n, w), F32)] * 2
    return pl.pallas_call(
        functools.partial(_rwkv_pre_kernel, tps, w),
        grid=(n // tm,),
        in_specs=[row(ca),
                  pl.BlockSpec((8, ca), lambda i: (jnp.maximum(i * hb - 1, 0), 0)),
                  pl.BlockSpec((8, ca), lambda i: (jnp.minimum((i + 1) * hb, nb8 - 1), 0)),
                  _full((1, ca)), _full((1, 2 * w)), _full((LANES, 2 * w)), _full((1, 2 * w)),
                  _full((LANES, 2 * w)), _full((LANES, w)), _full((1, w)), _full((1, w)), _full((1, w)),
                  _full((w, w))],
        out_specs=[row(w), row(w), row(w), row2, row2, row2, row(w), row(w)],
        out_shape=outs,
        compiler_params=_cparams("parallel"),
        name="rwkv_pre",
    )(pa, pa, pa, mu, w0, w2bd, a0, a2bd, g2, k_k, k_a, r_k, bd)


def _scan_kernel(r_ref, v_ref, a_ref, lw_ref, k_ref, b_ref, ms_ref, mi_ref, lc_ref, y_ref, s_scr):
    c = SCAN_CHUNK

    @pl.when(pl.program_id(2) == 0)
    def _():
        s_scr[...] = jnp.zeros_like(s_scr)

    lw = lw_ref[...]
    cum = _mm_hi(lc_ref[...], lw)
    tot = jnp.sum(lw, axis=0, keepdims=True)
    e_neg = jnp.exp(-cum)
    e_end = jnp.exp(tot - cum)
    p_end = jnp.exp(tot)
    a_t = a_ref[...] * jnp.exp(cum - lw)
    r_t = r_ref[...] * jnp.exp(cum)
    b_in = b_ref[...]
    k_in = k_ref[...]
    b_t = b_in * e_neg
    k_t = k_in * e_neg
    b_e = b_in * e_end
    k_e = k_in * e_end
    v_in = v_ref[...]

    strict = ms_ref[...] > 0.0
    incl = mi_ref[...] > 0.0
    first_head = lax.broadcasted_iota(jnp.int32, (1, LANES), 1) < HEAD_DIM

    def stack(x):
        return jnp.concatenate([jnp.where(first_head, x, 0.0), jnp.where(first_head, 0.0, x)], axis=0)

    n_groups = lw.shape[1] // LANES
    for m in range(n_groups):
        sl = slice(m * LANES, (m + 1) * LANES)
        a_s, r_s, b_s, k_s = stack(a_t[:, sl]), stack(r_t[:, sl]), stack(b_t[:, sl]), stack(k_t[:, sl])
        v_s = stack(v_in[:, sl])
        state = s_scr[m]

        l_ab = jnp.where(strict, _mm_nt_hi(a_s, b_s), 0.0)
        l_ak = jnp.where(strict, _mm_nt_hi(a_s, k_s), 0.0)
        l_rb = jnp.where(incl, _mm_nt_hi(r_s, b_s), 0.0)
        l_rk = jnp.where(incl, _mm_nt_hi(r_s, k_s), 0.0)

        x = _mm_nt_hi(a_s, state) + _mm_hi(l_ak, v_s)
        lp = l_ab
        x = x + _mm_hi(lp, x)
        span = 2
        while span < c:
            lp = _mm_hi(lp, lp)
            x = x + _mm_hi(lp, x)
            span *= 2
        u_s = x

        y_s = _mm_nt_hi(r_s, state) + _mm_hi(l_rb, u_s) + _mm_hi(l_rk, v_s)
        y_ref[:, sl] = y_s[:c] + y_s[c:]

        upd = _mm_tn_hi(jnp.concatenate([u_s, v_s], axis=0),
                        jnp.concatenate([stack(b_e[:, sl]), stack(k_e[:, sl])], axis=0))
        s_scr[m] = state * p_end[:, sl] + upd


def _scan(r, v, na, lw, kd, bv, batch, seq):
    n, w = r.shape
    c = SCAN_CHUNK
    nc = seq // c
    t = np.arange(c)
    before = [(t[None, :] < t[:, None]), (t[None, :] > t[:, None])]
    eye = np.eye(c, dtype=bool)
    tile2 = lambda m: np.tile(m, (2, 2))
    ms = jnp.asarray(np.stack([tile2(m) for m in before]).astype(np.float32))
    mi = jnp.asarray(np.stack([tile2(m | eye) for m in before]).astype(np.float32))
    lc = jnp.asarray(np.stack([(m | eye) for m in before]).astype(np.float32))

    def chunk_row(d, b, ci):
        return b * nc + ci + d * (nc - 1 - 2 * ci)

    shared = pl.BlockSpec((c, w), lambda d, b, ci: (chunk_row(d, b, ci), 0))
    per_dir = pl.BlockSpec((None, c, w), lambda d, b, ci: (d, chunk_row(d, b, ci), 0))
    by_dir = lambda s: pl.BlockSpec((None, s, s), lambda d, b, ci: (d, 0, 0))
    return pl.pallas_call(
        _scan_kernel,
        grid=(2, batch, nc),
        in_specs=[shared, shared, shared, per_dir, per_dir, per_dir, by_dir(2 * c), by_dir(2 * c), by_dir(c)],
        out_specs=per_dir,
        out_shape=jax.ShapeDtypeStruct((2, n, w), F32),
        scratch_shapes=[pltpu.VMEM((w // LANES, LANES, LANES), F32)],
        compiler_params=_cparams("parallel", "parallel", "arbitrary"),
        name="scan",
    )(r, v, na, lw, kd, bv, ms, mi, lc)


def _mixer_kernel(alpha, chunk,
                  x_ref, y_ref, g_ref, bonus_ref, pb_ref, pg_ref,
                  lng_ref, lnb_ref, gng_ref, gnb_ref, bd_ref, mlg_ref, mlb_ref, ws_ref, bs_ref,
                  wbr_ref, wmix_ref, l1g_ref, l1b_ref, o_ref):
    h0 = _ln(x_ref[...], lng_ref[...], lnb_ref[...])
    bd = bd_ref[...]
    inv = np.float32(1.0 / HEAD_DIM)

    y = y_ref[0] + y_ref[1]
    mu = _mm_hi(y, bd) * inv
    yc = y - mu
    var = _mm_hi(yc * yc, bd) * inv
    y_a = (yc * lax.rsqrt(var + GN_EPS) * gng_ref[...] + gnb_ref[...] + bonus_ref[...]) * g_ref[...]

    gp = _gelu(pb_ref[...])
    w = gp.shape[1] // 2
    u = gp[:, :w]
    vn = _ln(gp[:, w:], mlg_ref[...], mlb_ref[...])
    tm = u.shape[0]
    first_group = lax.broadcasted_iota(jnp.int32, (1, LANES), 1) < HEAD_DIM
    rows = []
    for ci in range(tm // chunk):
        vc = vn[ci * chunk:(ci + 1) * chunk]
        cols = []
        for m in range(w // LANES):
            vp = vc[:, m * LANES:(m + 1) * LANES]
            stacked = jnp.concatenate([jnp.where(first_group, vp, 0.0), jnp.where(first_group, 0.0, vp)], axis=0)
            cols.append(_mm(ws_ref[m], stacked))
        rows.append(jnp.concatenate(cols, axis=1) + bs_ref[...])
    sv = jnp.concatenate(rows, axis=0)
    y_b = u * sv

    pg = pg_ref[...]
    d = pg.shape[1] // 2
    merged = _sigmoid(pg[:, :d]) * _mm(y_a, wbr_ref[0]) + _sigmoid(pg[:, d:]) * _mm(y_b, wbr_ref[1])
    mix = _mm(merged, wmix_ref[...])
    o_ref[...] = _ln(alpha * h0 + mix, l1g_ref[...], l1b_ref[...])


def _mixer(x2, y, g, bonus, pb, pg, lng, lnb, gng, gnb, bd, mlg, mlb, wscat, bsfull, wbr, wmix, l1g, l1b,
           alpha, chunk, tm):
    n, d = x2.shape
    w = g.shape[1]
    row = lambda c: pl.BlockSpec((tm, c), lambda i: (i, 0))
    return pl.pallas_call(
        functools.partial(_mixer_kernel, alpha, chunk),
        grid=(n // tm,),
        in_specs=[row(d), pl.BlockSpec((2, tm, w), lambda i: (0, i, 0)), row(w), row(w), row(2 * w), row(2 * d),
                  _full((1, d)), _full((1, d)), _full((1, w)), _full((1, w)), _full((w, w)),
                  _full((1, w)), _full((1, w)), _full(wscat.shape), _full(bsfull.shape),
                  _full(wbr.shape), _full(wmix.shape), _full((1, d)), _full((1, d))],
        out_specs=row(d),
        out_shape=jax.ShapeDtypeStruct((n, d), F32),
        compiler_params=_cparams("parallel"),
        name="mixer",
    )(x2, y, g, bonus, pb, pg, lng, lnb, gng, gnb, bd, mlg, mlb, wscat, bsfull, wbr, wmix, l1g, l1b)


def _memkv_kernel(m_ref, g_ref, b_ref, w_ref, o_ref):
    o_ref[...] = _mm(_ln(m_ref[...], g_ref[...], b_ref[...]), w_ref[...]).astype(BF16)


def _memkv(mem2, g, b, wkv, tm):
    n, d = mem2.shape
    return pl.pallas_call(
        _memkv_kernel,
        grid=(n // tm,),
        in_specs=[pl.BlockSpec((tm, d), lambda i: (i, 0)), _full((1, d)), _full((1, d)), _full(wkv.shape)],
        out_specs=pl.BlockSpec((tm, wkv.shape[1]), lambda i: (i, 0)),
        out_shape=jax.ShapeDtypeStruct((n, wkv.shape[1]), BF16),
        compiler_params=_cparams("parallel"),
        name="memkv",
    )(mem2, g, b, wkv)


def _xattn_kernel(alpha, heads, h_ref, kv_ref, wq_ref, wo_ref, g_ref, b_ref, o_ref):
    h = h_ref[...]
    d = h.shape[1]
    hd = d // heads
    q = _mm(h, wq_ref[...])
    kv = kv_ref[...]
    scale = np.float32(hd ** -0.5)
    out = jnp.zeros_like(h)
    for i in range(heads):
        qh = q[:, i * hd:(i + 1) * hd].astype(BF16)
        kh = kv[:, i * hd:(i + 1) * hd]
        vh = kv[:, d + i * hd:d + (i + 1) * hd]
        s = lax.dot_general(qh, kh, (((1,), (1,)), ((), ())), preferred_element_type=F32) * scale
        s = s - jnp.max(s, axis=-1, keepdims=True)
        e = jnp.exp(s)
        prob = e / jnp.sum(e, axis=-1, keepdims=True)
        oh = _mm(prob, vh)
        out = out + _mm(oh, wo_ref[i * hd:(i + 1) * hd, :])
    o_ref[...] = _ln(alpha * h + out, g_ref[...], b_ref[...])


def _xattn(h1, kv, wq, wo, g, b, alpha, heads, seq, mem_len, tm):
    n, d = h1.shape
    tps = seq // tm
    return pl.pallas_call(
        functools.partial(_xattn_kernel, alpha, heads),
        grid=(n // tm,),
        in_specs=[pl.BlockSpec((tm, d), lambda i: (i, 0)),
                  pl.BlockSpec((mem_len, 2 * d), lambda i: (i // tps, 0)),
                  _full(wq.shape), _full(wo.shape), _full((1, d)), _full((1, d))],
        out_specs=pl.BlockSpec((tm, d), lambda i: (i, 0)),
        out_shape=jax.ShapeDtypeStruct((n, d), F32),
        compiler_params=_cparams("parallel"),
        name="xattn",
    )(h1, kv, wq, wo, g, b)


def _extract_topk(s, k):
    rows = s.shape[0]
    ridx = lax.broadcasted_iota(jnp.int32, s.shape, 0).astype(F32)
    rank = jnp.full(s.shape, float(k), F32)
    vals = []
    for j in range(k):
        m = jnp.max(s, axis=0, keepdims=True)
        first = jnp.min(jnp.where(s == m, ridx, float(rows)), axis=0, keepdims=True)
        sel = ridx == first
        rank = jnp.where(sel, float(j), rank)
        s = jnp.where(sel, -jnp.inf, s)
        vals.append(m)
    return vals, rank


def _route_kernel(heads, h_ref, wq_ref, sk_ref, cnt_ref, e1_ref, rank2_ref, e2_ref, q_scr):
    q_scr[...] = _mm_hi(h_ref[...], wq_ref[...])
    k = TOPK

    def per_head(h, carry):
        off = pl.multiple_of(h * (2 * LANES), 2 * LANES)
        s1 = _mm_nt_hi(sk_ref[0], q_scr[:, pl.ds(off, LANES)])
        s2 = _mm_nt_hi(sk_ref[1], q_scr[:, pl.ds(off + LANES, LANES)])
        top1, rank1 = _extract_topk(s1, k)
        top2, rank2 = _extract_topk(s2, k)
        t2 = jnp.concatenate(top2, axis=0)
        cand = jnp.concatenate([top1[i] + t2 for i in range(k)], axis=0)
        best, crank = _extract_topk(cand, k)
        chosen = jnp.where(crank < float(k), 1.0, 0.0)
        cnt_dense = jnp.zeros_like(s1)
        for i in range(k):
            n_i = jnp.sum(chosen[i * k:(i + 1) * k], axis=0, keepdims=True)
            cnt_dense = cnt_dense + jnp.where(rank1 == float(i), n_i, 0.0)
        z = best[0] * 0.0
        for j in range(k):
            z = z + jnp.exp(best[j] - best[0])
        cnt_ref[h] = cnt_dense
        e1_ref[h] = jnp.exp(s1 - top1[0]) / z
        rank2_ref[h] = rank2
        e2_ref[h] = jnp.exp(s2 - top2[0])
        return carry

    lax.fori_loop(0, heads, per_head, 0)


def _route(h2, wq, sk, heads, tt):
    n, d = h2.shape
    blk = pl.BlockSpec((heads, N_KEYS, tt), lambda i: (0, 0, i))
    return pl.pallas_call(
        functools.partial(_route_kernel, heads),
        grid=(n // tt,),
        in_specs=[pl.BlockSpec((tt, d), lambda i: (i, 0)), _full(wq.shape), _full(sk.shape)],
        out_specs=[blk] * 4,
        out_shape=[jax.ShapeDtypeStruct((heads, N_KEYS, n), F32)] * 4,
        scratch_shapes=[pltpu.VMEM((tt, wq.shape[1]), F32)],
        compiler_params=_cparams("parallel"),
        name="route",
    )(h2, wq, sk)


def _peer_kernel(alpha, heads, h_ref, u_ref, vt_ref, cnt_ref, e1_ref, rank2_ref, e2_ref, g_ref, b_ref,
                 o_ref, acc_scr, w_scr, hb_scr):
    e = pl.program_id(1)
    te = u_ref.shape[0]
    blocks = te // N_KEYS

    @pl.when(e == 0)
    def _():
        acc_scr[...] = jnp.zeros_like(acc_scr)
        hb_scr[...] = h_ref[...].astype(BF16)

    def per_block(il, carry):
        ig = e * blocks + il
        r0 = pl.multiple_of(il * N_KEYS, N_KEYS)
        act = lax.dot_general(u_ref[pl.ds(r0, N_KEYS), :], hb_scr[...], (((1,), (1,)), ((), ())),
                              preferred_element_type=F32)
        gate = jnp.zeros_like(act)
        for h in range(heads):
            cnt = cnt_ref[h, pl.ds(ig, 1), :]
            e1 = e1_ref[h, pl.ds(ig, 1), :]
            gate = gate + jnp.where(rank2_ref[h] < cnt, e2_ref[h], 0.0) * e1
        w_scr[pl.ds(r0, N_KEYS), :] = (gate * _gelu(act)).astype(BF16)
        return carry

    lax.fori_loop(0, blocks, per_block, 0)
    acc_scr[...] += jnp.dot(vt_ref[...], w_scr[...], preferred_element_type=F32)

    @pl.when(e == pl.num_programs(1) - 1)
    def _():
        y = jnp.transpose(acc_scr[...])
        o_ref[...] = _ln(alpha * h_ref[...] + y, g_ref[...], b_ref[...])


def _peer(h2, u_bf, vt_bf, cnt, e1, rank2, e2, g, b, alpha, heads, tt, te):
    n, d = h2.shape
    n_exp = u_bf.shape[0]
    blk = pl.BlockSpec((heads, N_KEYS, tt), lambda i, e: (0, 0, i))
    return pl.pallas_call(
        functools.partial(_peer_kernel, alpha, heads),
        grid=(n // tt, n_exp // te),
        in_specs=[pl.BlockSpec((tt, d), lambda i, e: (i, 0)),
                  pl.BlockSpec((te, d), lambda i, e: (e, 0)),
                  pl.BlockSpec((d, te), lambda i, e: (0, e)),
                  blk, blk, blk, blk, _full((1, d)), _full((1, d))],
        out_specs=pl.BlockSpec((tt, d), lambda i, e: (i, 0)),
        out_shape=jax.ShapeDtypeStruct((n, d), F32),
        scratch_shapes=[pltpu.VMEM((d, tt), F32), pltpu.VMEM((te, tt), BF16), pltpu.VMEM((tt, d), BF16)],
        compiler_params=_cparams("parallel", "arbitrary"),
        name="peer",
    )(h2, u_bf, vt_bf, cnt, e1, rank2, e2, g, b)


def _block_diag2(w):
    z = jnp.zeros_like(w[0])
    return jnp.concatenate([jnp.concatenate([w[0], z], axis=1), jnp.concatenate([z, w[1]], axis=1)], axis=0)


def _layer(h_in, mem2, batch, seq, mem_len, lng, lnb, first, p):
    del first
    n, d = h_in.shape
    (w_in, rwkv_mu, rwkv_w0, rwkv_w2, rwkv_a0, rwkv_a2, rwkv_g2, rwkv_k_k, rwkv_k_a, rwkv_r_k,
     rwkv_gn_g, rwkv_gn_b, gmlp_ln_g, gmlp_ln_b, gmlp_w_s, gmlp_b_s, w_branch, w_mix_out, ln1_g, ln1_b,
     mem_ln_g, mem_ln_b, xattn_w_q, xattn_w_kv, xattn_w_o, ln2_g, ln2_b,
     peer_w_query, peer_sub_keys, peer_u, peer_v, ln3_g, ln3_b, alpha) = p
    width = rwkv_k_k.shape[0]
    rw_cols = rwkv_mu.shape[0]
    gm_cols = 2 * gmlp_ln_g.shape[0]
    chunk = gmlp_w_s.shape[1]
    row1 = lambda t: t.reshape(1, -1)

    tm = 256
    wa = w_in[:, :rw_cols].astype(BF16)
    wb = w_in[:, rw_cols:rw_cols + gm_cols].astype(BF16)
    wg = w_in[:, rw_cols + gm_cols:].astype(BF16)
    pa, pb, pg = _inproj(h_in, row1(lng), row1(lnb), wa, wb, wg, tm)

    heads_r = width // HEAD_DIM
    bd = jnp.asarray(np.kron(np.eye(heads_r, dtype=np.float32), np.ones((HEAD_DIM, HEAD_DIM), np.float32)))
    r, v, na, lw, kd, bv, g, bonus = _rwkv_pre(
        pa, seq, row1(rwkv_mu), row1(rwkv_w0), _block_diag2(rwkv_w2), row1(rwkv_a0), _block_diag2(rwkv_a2),
        rwkv_g2, row1(rwkv_k_k), row1(rwkv_k_a), row1(rwkv_r_k), bd, tm)
    y = _scan(r, v, na, lw, kd, bv, batch, seq)

    groups = gmlp_w_s.shape[0]
    wscat = jnp.concatenate([gmlp_w_s[0::2], gmlp_w_s[1::2]], axis=2).astype(BF16)
    bsfull = jnp.repeat(gmlp_b_s.T, HEAD_DIM, axis=1)
    assert groups * HEAD_DIM == width
    h1 = _mixer(h_in, y, g, bonus, pb, pg, row1(lng), row1(lnb), row1(rwkv_gn_g), row1(rwkv_gn_b), bd,
                row1(gmlp_ln_g), row1(gmlp_ln_b), wscat, bsfull, w_branch.astype(BF16), w_mix_out.astype(BF16),
                row1(ln1_g), row1(ln1_b), alpha, chunk, tm)

    kv = _memkv(mem2, row1(mem_ln_g), row1(mem_ln_b), xattn_w_kv.astype(BF16), mem_len)
    xheads = 4
    h2 = _xattn(h1, kv, xattn_w_q.astype(BF16), xattn_w_o.astype(BF16), row1(ln2_g), row1(ln2_b),
                alpha, xheads, seq, mem_len, tm)

    pheads = peer_w_query.shape[1] // (2 * LANES)
    tt = 256
    cnt, e1, rank2, e2 = _route(h2, peer_w_query, peer_sub_keys, pheads, tt)
    h3 = _peer(h2, peer_u.astype(BF16), peer_v.T.astype(BF16), cnt, e1, rank2, e2, row1(ln3_g), row1(ln3_b),
               alpha, pheads, tt, 2048)
    return h3


def kernel(x, mem, ln_emb_g, ln_emb_b, w_in, rwkv_mu, rwkv_w0, rwkv_w2, rwkv_a0, rwkv_a2, rwkv_g2, rwkv_k_k, rwkv_k_a, rwkv_r_k, rwkv_gn_g, rwkv_gn_b, gmlp_ln_g, gmlp_ln_b, gmlp_w_s, gmlp_b_s, w_branch, w_mix_out, ln1_g, ln1_b, mem_ln_g, mem_ln_b, xattn_w_q, xattn_w_kv, xattn_w_o, ln2_g, ln2_b, peer_w_query, peer_sub_keys, peer_u, peer_v, ln3_g, ln3_b):
    batch, seq, d = x.shape
    mem_len = mem.shape[1]
    depth = w_in.shape[0]
    assert depth == 1, "the layer pipeline applies the embedding LN inside the first layer only"
    alpha = float((2.0 * depth) ** 0.25)
    x2 = x.reshape(batch * seq, d)
    mem2 = mem.reshape(batch * mem_len, d)
    flat = lambda t: t.reshape(-1)
    p = (w_in[0], rwkv_mu[0], flat(rwkv_w0[0]), rwkv_w2[0], flat(rwkv_a0[0]), rwkv_a2[0], rwkv_g2[0],
         rwkv_k_k[0], rwkv_k_a[0], flat(rwkv_r_k[0]), rwkv_gn_g[0], rwkv_gn_b[0], gmlp_ln_g[0], gmlp_ln_b[0],
         gmlp_w_s[0], gmlp_b_s[0], w_branch[0], w_mix_out[0], ln1_g[0], ln1_b[0], mem_ln_g[0], mem_ln_b[0],
         xattn_w_q[0], xattn_w_kv[0], xattn_w_o[0], ln2_g[0], ln2_b[0], peer_w_query[0], peer_sub_keys[0],
         peer_u[0], peer_v[0], ln3_g[0], ln3_b[0], alpha)
    h = _layer(x2, mem2, batch, seq, mem_len, ln_emb_g, ln_emb_b, True, p)
    return h.reshape(batch, seq, d)
```

```python
import functools

import numpy as np
import jax
import jax.numpy as jnp
from jax import lax
from jax.experimental import pallas as pl
from jax.experimental.pallas import tpu as pltpu

F32 = jnp.float32
BF16 = jnp.bfloat16
HIGHEST = lax.Precision.HIGHEST

LN_EPS = 1e-5
GN_EPS = 64e-5
LANES = 128
HEAD_DIM = 64
SCAN_CHUNK = 64
TOPK = 16
N_KEYS = 128
VMEM_LIMIT = 56 * 1024 * 1024


def _cparams(*sem):
    return pltpu.CompilerParams(dimension_semantics=sem, vmem_limit_bytes=VMEM_LIMIT)


def _ln(x, g, b, eps=LN_EPS):
    mu = jnp.mean(x, axis=-1, keepdims=True)
    xc = x - mu
    var = jnp.mean(xc * xc, axis=-1, keepdims=True)
    return xc * lax.rsqrt(var + eps) * g + b


def _mm(a, b):
    return jnp.dot(a.astype(BF16), b.astype(BF16), preferred_element_type=F32)


def _mm_hi(a, b):
    return jnp.dot(a, b, precision=HIGHEST, preferred_element_type=F32)


def _mm_nt(a, b):
    return lax.dot_general(a, b, (((1,), (1,)), ((), ())), preferred_element_type=F32)


def _mm_nt_hi(a, b):
    return lax.dot_general(a, b, (((1,), (1,)), ((), ())), precision=HIGHEST,
                           preferred_element_type=F32)


def _mm_tn_hi(a, b):
    return lax.dot_general(a, b, (((0,), (0,)), ((), ())), precision=HIGHEST,
                           preferred_element_type=F32)


def _gelu(x):
    return 0.5 * x * (1.0 + lax.erf(x * np.float32(1.0 / np.sqrt(2.0))))


def _sigmoid(x):
    return 1.0 / (1.0 + jnp.exp(-x))


def _full(shape):
    nd = len(shape)
    return pl.BlockSpec(shape, lambda *_: (0,) * nd)


def _inproj_kernel(x_ref, g_ref, b_ref, wa_ref, wb_ref, wg_ref, pa_ref, pb_ref, pg_ref):
    h = _ln(x_ref[...], g_ref[...], b_ref[...]).astype(BF16)
    pa_ref[...] = jnp.dot(h, wa_ref[...], preferred_element_type=F32)
    pb_ref[...] = jnp.dot(h, wb_ref[...], preferred_element_type=F32)
    pg_ref[...] = jnp.dot(h, wg_ref[...], preferred_element_type=F32)


def _inproj(x2, g, b, wa, wb, wg, tm):
    n, d = x2.shape
    ca, cb, cg = wa.shape[1], wb.shape[1], wg.shape[1]
    row = lambda c: pl.BlockSpec((tm, c), lambda i: (i, 0))
    return pl.pallas_call(
        _inproj_kernel,
        grid=(n // tm,),
        in_specs=[row(d), _full((1, d)), _full((1, d)), _full((d, ca)), _full((d, cb)), _full((d, cg))],
        out_specs=[row(ca), row(cb), row(cg)],
        out_shape=[jax.ShapeDtypeStruct((n, c), F32) for c in (ca, cb, cg)],
        compiler_params=_cparams("parallel"),
        name="inproj",
    )(x2, g, b, wa, wb, wg)


def _rwkv_pre_kernel(tiles_per_seq, width,
                     p_ref, prev_ref, next_ref, mu_ref, w0_ref, w2_ref, a0_ref, a2_ref, g2_ref,
                     kk_ref, ka_ref, rk_ref, bd_ref,
                     r_out, v_out, na_out, lw_out, kd_out, bv_out, g_out, bonus_out):
    i = pl.program_id(0)
    pos = i % tiles_per_seq
    p = p_ref[...]
    tm = p.shape[0]
    row = lax.broadcasted_iota(jnp.int32, (tm, 1), 0)
    prev_row = jnp.where(pos == 0, 0.0, prev_ref[7:8, :])
    next_row = jnp.where(pos == tiles_per_seq - 1, 0.0, next_ref[0:1, :])
    prev = jnp.where(row == 0, prev_row, pltpu.roll(p, 1, 0))
    nxt = jnp.where(row == tm - 1, next_row, pltpu.roll(p, tm - 1, 0))
    ps = p + mu_ref[...] * (0.5 * (prev + nxt) - p)

    w = width
    r = ps[:, 0:w]
    k = ps[:, w:2 * w]
    v = ps[:, 2 * w:3 * w]
    wd = ps[:, 3 * w:3 * w + LANES]
    ad = ps[:, 3 * w + LANES:3 * w + 2 * LANES]
    gd = ps[:, 3 * w + 2 * LANES:3 * w + 3 * LANES]

    wz = w0_ref[...] + _mm_hi(jnp.tanh(wd), w2_ref[...])
    w_log = -(jnp.maximum(-wz, 0.0) + jnp.log(1.0 + jnp.exp(-jnp.abs(wz)))) - 0.5
    log_decay = -jnp.exp(w_log)
    a = _sigmoid(a0_ref[...] + _mm_hi(ad, a2_ref[...]))
    g = _mm_hi(_sigmoid(gd), g2_ref[...])

    bd = bd_ref[...]
    kk = k * kk_ref[...]
    nrm = jnp.sqrt(_mm_hi(kk * kk, bd))
    kk = kk / jnp.maximum(nrm, 1e-12)

    ka = ka_ref[...]
    k0 = k * (1.0 + (a[:, 0:w] - 1.0) * ka)
    k1 = k * (1.0 + (a[:, w:2 * w] - 1.0) * ka)
    rk = _mm_hi(r * (k0 + k1) * rk_ref[...], bd)

    r_out[...] = r
    v_out[...] = v
    na_out[...] = -kk
    lw_out[0] = log_decay[:, 0:w]
    lw_out[1] = log_decay[:, w:2 * w]
    kd_out[0] = k0
    kd_out[1] = k1
    bv_out[0] = kk * a[:, 0:w]
    bv_out[1] = kk * a[:, w:2 * w]
    g_out[...] = g
    bonus_out[...] = rk * v


def _rwkv_pre(pa, seq, mu, w0, w2bd, a0, a2bd, g2, k_k, k_a, r_k, bd, tm):
    n, ca = pa.shape
    w = bd.shape[0]
    tps = seq // tm
    hb = tm // 8
    nb8 = n // 8
    row = lambda c: pl.BlockSpec((tm, c), lambda i: (i, 0))
    row2 = pl.BlockSpec((2, tm, w), lambda i: (0, i, 0))
    outs = [jax.ShapeDtypeStruct((n, w), F32)] * 3 + [jax.ShapeDtypeStruct((2, n, w), F32)] * 3 + \
           [jax.ShapeDtypeStruct((n, w), F32)] * 2
    return pl.pallas_call(
        functools.partial(_rwkv_pre_kernel, tps, w),
        grid=(n // tm,),
        in_specs=[row(ca),
                  pl.BlockSpec((8, ca), lambda i: (jnp.maximum(i * hb - 1, 0), 0)),
                  pl.BlockSpec((8, ca), lambda i: (jnp.minimum((i + 1) * hb, nb8 - 1), 0)),
                  _full((1, ca)), _full((1, 2 * w)), _full((LANES, 2 * w)), _full((1, 2 * w)),
                  _full((LANES, 2 * w)), _full((LANES, w)), _full((1, w)), _full((1, w)), _full((1, w)),
                  _full((w, w))],
        out_specs=[row(w), row(w), row(w), row2, row2, row2, row(w), row(w)],
        out_shape=outs,
        compiler_params=_cparams("parallel"),
        name="rwkv_pre",
    )(pa, pa, pa, mu, w0, w2bd, a0, a2bd, g2, k_k, k_a, r_k, bd)


def _scan_kernel(r_ref, v_ref, a_ref, lw_ref, k_ref, b_ref, ms_ref, mi_ref, lc_ref, y_ref, s_scr):
    c = SCAN_CHUNK

    @pl.when(pl.program_id(2) == 0)
    def _():
        s_scr[...] = jnp.zeros_like(s_scr)

    lw = lw_ref[...]
    cum = _mm_hi(lc_ref[...], lw)
    tot = jnp.sum(lw, axis=0, keepdims=True)
    e_neg = jnp.exp(-cum)
    e_end = jnp.exp(tot - cum)
    p_end = jnp.exp(tot)
    a_t = a_ref[...] * jnp.exp(cum - lw)
    r_t = r_ref[...] * jnp.exp(cum)
    b_in = b_ref[...]
    k_in = k_ref[...]
    b_t = b_in * e_neg
    k_t = k_in * e_neg
    b_e = b_in * e_end
    k_e = k_in * e_end
    v_in = v_ref[...]

    strict = ms_ref[...] > 0.0
    incl = mi_ref[...] > 0.0
    first_head = lax.broadcasted_iota(jnp.int32, (1, LANES), 1) < HEAD_DIM

    def stack(x):
        return jnp.concatenate([jnp.where(first_head, x, 0.0), jnp.where(first_head, 0.0, x)], axis=0)

    n_groups = lw.shape[1] // LANES
    h2 = 2 * c
    gs = range(n_groups)
    sls = [slice(m * LANES, (m + 1) * LANES) for m in gs]
    dot = functools.partial(jnp.dot, preferred_element_type=F32)
    ar = [jnp.concatenate([stack(a_t[:, s]), stack(r_t[:, s])], axis=0).astype(BF16) for s in sls]
    bk = [jnp.concatenate([stack(b_t[:, s]), stack(k_t[:, s])], axis=0).astype(BF16) for s in sls]
    v_s = [stack(v_in[:, s]) for s in sls]
    v_b = [x.astype(BF16) for x in v_s]
    state = [s_scr[m] for m in gs]

    l_all = [_mm_nt(ar[m], bk[m]) for m in gs]
    from_state = [_mm_nt(ar[m], state[m].astype(BF16)) for m in gs]
    l_ab = [jnp.where(strict, l[:h2, :h2], 0.0).astype(BF16) for l in l_all]
    l_k = [jnp.concatenate([jnp.where(strict, l[:h2, h2:], 0.0), jnp.where(incl, l[h2:, h2:], 0.0)],
                           axis=0).astype(BF16) for l in l_all]
    l_rb = [jnp.where(incl, l[h2:, :h2], 0.0).astype(BF16) for l in l_all]
    from_v = [dot(l_k[m], v_b[m]) for m in gs]

    x = [from_state[m][:h2] + from_v[m][:h2] for m in gs]
    lp = l_ab
    x = [x[m] + dot(lp[m], x[m].astype(BF16)) for m in gs]
    span = 2
    while span < c:
        lp = [dot(lp[m], lp[m]).astype(BF16) for m in gs]
        x = [x[m] + dot(lp[m], x[m].astype(BF16)) for m in gs]
        span *= 2
    u_s = x

    y_s = [from_state[m][h2:] + from_v[m][h2:] + dot(l_rb[m], u_s[m].astype(BF16)) for m in gs]
    uv_t = [jnp.transpose(jnp.concatenate([u_s[m], v_s[m]], axis=0)).astype(BF16) for m in gs]
    bk_e = [jnp.concatenate([stack(b_e[:, s]), stack(k_e[:, s])], axis=0).astype(BF16) for s in sls]
    new_state = [state[m] * p_end[:, sls[m]] + dot(uv_t[m], bk_e[m]) for m in gs]
    for m in gs:
        y_ref[:, sls[m]] = y_s[m][:c] + y_s[m][c:]
        s_scr[m] = new_state[m]


def _scan(r, v, na, lw, kd, bv, batch, seq):
    n, w = r.shape
    c = SCAN_CHUNK
    nc = seq // c
    t = np.arange(c)
    before = [(t[None, :] < t[:, None]), (t[None, :] > t[:, None])]
    eye = np.eye(c, dtype=bool)
    tile2 = lambda m: np.tile(m, (2, 2))
    ms = jnp.asarray(np.stack([tile2(m) for m in before]).astype(np.float32))
    mi = jnp.asarray(np.stack([tile2(m | eye) for m in before]).astype(np.float32))
    lc = jnp.asarray(np.stack([(m | eye) for m in before]).astype(np.float32))

    def chunk_row(d, b, ci):
        return b * nc + ci + d * (nc - 1 - 2 * ci)

    shared = pl.BlockSpec((c, w), lambda d, b, ci: (chunk_row(d, b, ci), 0))
    per_dir = pl.BlockSpec((None, c, w), lambda d, b, ci: (d, chunk_row(d, b, ci), 0))
    by_dir = lambda s: pl.BlockSpec((None, s, s), lambda d, b, ci: (d, 0, 0))
    return pl.pallas_call(
        _scan_kernel,
        grid=(2, batch, nc),
        in_specs=[shared, shared, shared, per_dir, per_dir, per_dir, by_dir(2 * c), by_dir(2 * c), by_dir(c)],
        out_specs=per_dir,
        out_shape=jax.ShapeDtypeStruct((2, n, w), F32),
        scratch_shapes=[pltpu.VMEM((w // LANES, LANES, LANES), F32)],
        compiler_params=_cparams("parallel", "parallel", "arbitrary"),
        name="scan",
    )(r, v, na, lw, kd, bv, ms, mi, lc)


def _mixer_kernel(alpha, chunk,
                  x_ref, y_ref, g_ref, bonus_ref, pb_ref, pg_ref,
                  lng_ref, lnb_ref, gng_ref, gnb_ref, bd_ref, mlg_ref, mlb_ref, ws_ref, bs_ref,
                  wbr_ref, wmix_ref, l1g_ref, l1b_ref, o_ref):
    h0 = _ln(x_ref[...], lng_ref[...], lnb_ref[...])
    bd = bd_ref[...]
    inv = np.float32(1.0 / HEAD_DIM)

    y = y_ref[0] + y_ref[1]
    mu = _mm_hi(y, bd) * inv
    yc = y - mu
    var = _mm_hi(yc * yc, bd) * inv
    y_a = (yc * lax.rsqrt(var + GN_EPS) * gng_ref[...] + gnb_ref[...] + bonus_ref[...]) * g_ref[...]

    gp = _gelu(pb_ref[...])
    w = gp.shape[1] // 2
    u = gp[:, :w]
    vn = _ln(gp[:, w:], mlg_ref[...], mlb_ref[...])
    tm = u.shape[0]
    first_group = lax.broadcasted_iota(jnp.int32, (1, LANES), 1) < HEAD_DIM
    rows = []
    for ci in range(tm // chunk):
        vc = vn[ci * chunk:(ci + 1) * chunk]
        cols = []
        for m in range(w // LANES):
            vp = vc[:, m * LANES:(m + 1) * LANES]
            stacked = jnp.concatenate([jnp.where(first_group, vp, 0.0), jnp.where(first_group, 0.0, vp)], axis=0)
            cols.append(_mm(ws_ref[m], stacked))
        rows.append(jnp.concatenate(cols, axis=1) + bs_ref[...])
    sv = jnp.concatenate(rows, axis=0)
    y_b = u * sv

    pg = pg_ref[...]
    d = pg.shape[1] // 2
    merged = _sigmoid(pg[:, :d]) * _mm(y_a, wbr_ref[0]) + _sigmoid(pg[:, d:]) * _mm(y_b, wbr_ref[1])
    mix = _mm(merged, wmix_ref[...])
    o_ref[...] = _ln(alpha * h0 + mix, l1g_ref[...], l1b_ref[...])


def _mixer(x2, y, g, bonus, pb, pg, lng, lnb, gng, gnb, bd, mlg, mlb, wscat, bsfull, wbr, wmix, l1g, l1b,
           alpha, chunk, tm):
    n, d = x2.shape
    w = g.shape[1]
    row = lambda c: pl.BlockSpec((tm, c), lambda i: (i, 0))
    return pl.pallas_call(
        functools.partial(_mixer_kernel, alpha, chunk),
        grid=(n // tm,),
        in_specs=[row(d), pl.BlockSpec((2, tm, w), lambda i: (0, i, 0)), row(w), row(w), row(2 * w), row(2 * d),
                  _full((1, d)), _full((1, d)), _full((1, w)), _full((1, w)), _full((w, w)),
                  _full((1, w)), _full((1, w)), _full(wscat.shape), _full(bsfull.shape),
                  _full(wbr.shape), _full(wmix.shape), _full((1, d)), _full((1, d))],
        out_specs=row(d),
        out_shape=jax.ShapeDtypeStruct((n, d), F32),
        compiler_params=_cparams("parallel"),
        name="mixer",
    )(x2, y, g, bonus, pb, pg, lng, lnb, gng, gnb, bd, mlg, mlb, wscat, bsfull, wbr, wmix, l1g, l1b)


def _memkv_kernel(m_ref, g_ref, b_ref, w_ref, o_ref):
    o_ref[...] = _mm(_ln(m_ref[...], g_ref[...], b_ref[...]), w_ref[...]).astype(BF16)


def _memkv(mem2, g, b, wkv, tm):
    n, d = mem2.shape
    return pl.pallas_call(
        _memkv_kernel,
        grid=(n // tm,),
        in_specs=[pl.BlockSpec((tm, d), lambda i: (i, 0)), _full((1, d)), _full((1, d)), _full(wkv.shape)],
        out_specs=pl.BlockSpec((tm, wkv.shape[1]), lambda i: (i, 0)),
        out_shape=jax.ShapeDtypeStruct((n, wkv.shape[1]), BF16),
        compiler_params=_cparams("parallel"),
        name="memkv",
    )(mem2, g, b, wkv)


def _xattn_kernel(alpha, heads, h_ref, kv_ref, wq_ref, wo_ref, g_ref, b_ref, o_ref):
    h = h_ref[...]
    d = h.shape[1]
    hd = d // heads
    q = _mm(h, wq_ref[...])
    kv = kv_ref[...]
    scale = np.float32(hd ** -0.5)
    out = jnp.zeros_like(h)
    for i in range(heads):
        qh = q[:, i * hd:(i + 1) * hd].astype(BF16)
        kh = kv[:, i * hd:(i + 1) * hd]
        vh = kv[:, d + i * hd:d + (i + 1) * hd]
        s = lax.dot_general(qh, kh, (((1,), (1,)), ((), ())), preferred_element_type=F32) * scale
        s = s - jnp.max(s, axis=-1, keepdims=True)
        e = jnp.exp(s)
        prob = e / jnp.sum(e, axis=-1, keepdims=True)
        oh = _mm(prob, vh)
        out = out + _mm(oh, wo_ref[i * hd:(i + 1) * hd, :])
    o_ref[...] = _ln(alpha * h + out, g_ref[...], b_ref[...])


def _xattn(h1, kv, wq, wo, g, b, alpha, heads, seq, mem_len, tm):
    n, d = h1.shape
    tps = seq // tm
    return pl.pallas_call(
        functools.partial(_xattn_kernel, alpha, heads),
        grid=(n // tm,),
        in_specs=[pl.BlockSpec((tm, d), lambda i: (i, 0)),
                  pl.BlockSpec((mem_len, 2 * d), lambda i: (i // tps, 0)),
                  _full(wq.shape), _full(wo.shape), _full((1, d)), _full((1, d))],
        out_specs=pl.BlockSpec((tm, d), lambda i: (i, 0)),
        out_shape=jax.ShapeDtypeStruct((n, d), F32),
        compiler_params=_cparams("parallel"),
        name="xattn",
    )(h1, kv, wq, wo, g, b)


def _extract_topk(s, pos, k):
    rank = jnp.full(s.shape, float(k), F32)
    vals = []
    for j in range(k):
        m = jnp.max(s, axis=0, keepdims=True)
        first = jnp.min(jnp.where(s == m, pos, np.float32(np.inf)), axis=0, keepdims=True)
        sel = pos == first
        rank = jnp.where(sel, float(j), rank)
        s = jnp.where(sel, -jnp.inf, s)
        vals.append(m)
    return vals, rank


def _candidate_layout(k):
    pieces = [("row", 0, 0, 16), ("row", 1, 0, 8), ("col", 0, 0, 16), ("col", 1, 0, 8),
              ("col", 2, 0, 8), ("col", 3, 0, 8), ("col", 4, 0, 8)]
    pos, seen = [], set()
    for kind, fixed, start, n in pieces:
        for o in range(start, start + n):
            i, j = (fixed, o) if kind == "row" else (o, fixed)
            ok = (i + 1) * (j + 1) <= k and (i, j) not in seen
            if ok:
                seen.add((i, j))
            pos.append(float(i * k + j) if ok else -1.0)
    assert len(seen) == sum(k // (i + 1) for i in range(k))
    return pieces, np.asarray(pos, np.float32)


def _route_kernel(heads, h_ref, wqh_ref, wql_ref, sk_ref, cpos_ref, cnt_ref, e1_ref, rank2_ref, e2_ref, q_scr):
    h = h_ref[...]
    h_hi = h.astype(BF16)
    h_lo = (h - h_hi.astype(F32)).astype(BF16)
    q_scr[...] = (jnp.dot(h_hi, wqh_ref[...], preferred_element_type=F32)
                  + jnp.dot(h_hi, wql_ref[...], preferred_element_type=F32)
                  + jnp.dot(h_lo, wqh_ref[...], preferred_element_type=F32))
    k = TOPK
    pieces, _ = _candidate_layout(k)
    cpos = cpos_ref[...]
    cvalid = cpos >= 0.0
    key_pos = lax.broadcasted_iota(jnp.int32, (N_KEYS, h.shape[0]), 0).astype(F32)
    row16 = lax.broadcasted_iota(jnp.int32, (k, 1), 0)

    def per_head(hd, carry):
        off = pl.multiple_of(hd * (2 * LANES), 2 * LANES)
        s1 = _mm_nt_hi(sk_ref[0], q_scr[:, pl.ds(off, LANES)])
        s2 = _mm_nt_hi(sk_ref[1], q_scr[:, pl.ds(off + LANES, LANES)])
        top1, rank1 = _extract_topk(s1, key_pos, k)
        top2, rank2 = _extract_topk(s2, key_pos, k)
        t1 = jnp.concatenate(top1, axis=0)
        t2 = jnp.concatenate(top2, axis=0)
        parts = []
        for kind, fixed, start, n in pieces:
            if kind == "row":
                parts.append(top1[fixed] + t2[start:start + n])
            else:
                parts.append(t1[start:start + n] + top2[fixed])
        cand = jnp.where(cvalid, jnp.concatenate(parts, axis=0), -jnp.inf)
        best, crank = _extract_topk(cand, jnp.where(cvalid, cpos, np.float32(np.inf)), k)
        chosen = jnp.where(crank < float(k), 1.0, 0.0)
        n16 = jnp.zeros((k, h.shape[0]), F32)
        r0 = 0
        for kind, fixed, start, n in pieces:
            c = chosen[r0:r0 + n]
            r0 += n
            if kind == "row":
                n16 = n16 + jnp.where(row16 == fixed, jnp.sum(c, axis=0, keepdims=True), 0.0)
            elif n == k:
                n16 = n16 + c
            else:
                n16 = n16 + jnp.concatenate([c, jnp.zeros((k - n, h.shape[0]), F32)], axis=0)
        cnt_dense = jnp.zeros_like(s1)
        for i in range(k):
            cnt_dense = cnt_dense + jnp.where(rank1 == float(i), n16[i:i + 1], 0.0)
        z = jnp.zeros_like(best[0])
        for j in range(k):
            z = z + jnp.exp(best[j] - best[0])
        cnt_ref[hd] = cnt_dense
        e1_ref[hd] = jnp.exp(s1 - top1[0]) / z
        rank2_ref[hd] = rank2.astype(BF16)
        e2_ref[hd] = jnp.exp(s2 - top2[0]).astype(BF16)
        return carry

    lax.fori_loop(0, heads, per_head, 0)


def _route(h2, wq, sk, heads, tt):
    n, d = h2.shape
    wq_hi = wq.astype(BF16)
    wq_lo = (wq - wq_hi.astype(F32)).astype(BF16)
    _, pos = _candidate_layout(TOPK)
    cpos = jnp.asarray(np.broadcast_to(pos[:, None], (pos.shape[0], tt)).copy())
    blk = pl.BlockSpec((heads, N_KEYS, tt), lambda i: (0, 0, i))
    return pl.pallas_call(
        functools.partial(_route_kernel, heads),
        grid=(n // tt,),
        in_specs=[pl.BlockSpec((tt, d), lambda i: (i, 0)), _full(wq.shape), _full(wq.shape), _full(sk.shape),
                  _full(cpos.shape)],
        out_specs=[blk] * 4,
        out_shape=[jax.ShapeDtypeStruct((heads, N_KEYS, n), F32)] * 2
                  + [jax.ShapeDtypeStruct((heads, N_KEYS, n), BF16)] * 2,
        scratch_shapes=[pltpu.VMEM((tt, wq.shape[1]), F32)],
        compiler_params=_cparams("parallel"),
        name="route",
    )(h2, wq_hi, wq_lo, sk, cpos)


def _peer_kernel(alpha, heads, h_ref, u_ref, vt_ref, cnt_ref, e1_ref, rank2_ref, e2_ref, g_ref, b_ref,
                 o_ref, acc_scr, w_scr, hb_scr):
    e = pl.program_id(1)
    te = u_ref.shape[0]
    tt = h_ref.shape[0]
    blocks = te // N_KEYS
    sub = 16
    grp = N_KEYS // sub

    @pl.when(e == 0)
    def _():
        acc_scr[...] = jnp.zeros_like(acc_scr)
        hb_scr[...] = h_ref[...].astype(BF16)

    def one_block(il):
        ig = e * blocks + il
        r0 = pl.multiple_of(il * N_KEYS, N_KEYS)
        act = lax.dot_general(u_ref[pl.ds(r0, N_KEYS), :], hb_scr[...], (((1,), (1,)), ((), ())),
                              preferred_element_type=F32)
        gate = jnp.zeros((grp, sub, tt), BF16)
        for hd in range(heads):
            cnt = jnp.broadcast_to(cnt_ref[hd, pl.ds(ig, 1), :], (sub, tt)).astype(BF16)
            e1 = jnp.broadcast_to(e1_ref[hd, pl.ds(ig, 1), :], (sub, tt)).astype(BF16)
            rank2 = rank2_ref[hd].reshape(grp, sub, tt)
            e2 = e2_ref[hd].reshape(grp, sub, tt)
            gate = gate + jnp.where(rank2 < cnt[None], e2, jnp.zeros_like(e2)) * e1[None]
        w = gate * _gelu(act).astype(BF16).reshape(grp, sub, tt)
        w_scr[pl.ds(r0, N_KEYS), :] = w.reshape(N_KEYS, tt)

    def per_pair(ip, carry):
        one_block(2 * ip)
        one_block(2 * ip + 1)
        return carry

    lax.fori_loop(0, blocks // 2, per_pair, 0)
    acc_scr[...] += jnp.dot(vt_ref[...], w_scr[...], preferred_element_type=F32)

    @pl.when(e == pl.num_programs(1) - 1)
    def _():
        y = jnp.transpose(acc_scr[...])
        o_ref[...] = _ln(alpha * h_ref[...] + y, g_ref[...], b_ref[...])


def _peer(h2, u_bf, vt_bf, cnt, e1, rank2, e2, g, b, alpha, heads, tt, te):
    n, d = h2.shape
    n_exp = u_bf.shape[0]
    blk = pl.BlockSpec((heads, N_KEYS, tt), lambda i, e: (0, 0, i))
    return pl.pallas_call(
        functools.partial(_peer_kernel, alpha, heads),
        grid=(n // tt, n_exp // te),
        in_specs=[pl.BlockSpec((tt, d), lambda i, e: (i, 0)),
                  pl.BlockSpec((te, d), lambda i, e: (e, 0)),
                  pl.BlockSpec((d, te), lambda i, e: (0, e)),
                  blk, blk, blk, blk, _full((1, d)), _full((1, d))],
        out_specs=pl.BlockSpec((tt, d), lambda i, e: (i, 0)),
        out_shape=jax.ShapeDtypeStruct((n, d), F32),
        scratch_shapes=[pltpu.VMEM((d, tt), F32), pltpu.VMEM((te, tt), BF16), pltpu.VMEM((tt, d), BF16)],
        compiler_params=_cparams("parallel", "arbitrary"),
        name="peer",
    )(h2, u_bf, vt_bf, cnt, e1, rank2, e2, g, b)


def _block_diag2(w):
    z = jnp.zeros_like(w[0])
    return jnp.concatenate([jnp.concatenate([w[0], z], axis=1), jnp.concatenate([z, w[1]], axis=1)], axis=0)


def _layer(h_in, mem2, batch, seq, mem_len, lng, lnb, first, p):
    del first
    n, d = h_in.shape
    (w_in, rwkv_mu, rwkv_w0, rwkv_w2, rwkv_a0, rwkv_a2, rwkv_g2, rwkv_k_k, rwkv_k_a, rwkv_r_k,
     rwkv_gn_g, rwkv_gn_b, gmlp_ln_g, gmlp_ln_b, gmlp_w_s, gmlp_b_s, w_branch, w_mix_out, ln1_g, ln1_b,
     mem_ln_g, mem_ln_b, xattn_w_q, xattn_w_kv, xattn_w_o, ln2_g, ln2_b,
     peer_w_query, peer_sub_keys, peer_u, peer_v, ln3_g, ln3_b, alpha) = p
    width = rwkv_k_k.shape[0]
    rw_cols = rwkv_mu.shape[0]
    gm_cols = 2 * gmlp_ln_g.shape[0]
    chunk = gmlp_w_s.shape[1]
    row1 = lambda t: t.reshape(1, -1)

    tm = 256
    wa = w_in[:, :rw_cols].astype(BF16)
    wb = w_in[:, rw_cols:rw_cols + gm_cols].astype(BF16)
    wg = w_in[:, rw_cols + gm_cols:].astype(BF16)
    pa, pb, pg = _inproj(h_in, row1(lng), row1(lnb), wa, wb, wg, tm)

    heads_r = width // HEAD_DIM
    bd = jnp.asarray(np.kron(np.eye(heads_r, dtype=np.float32), np.ones((HEAD_DIM, HEAD_DIM), np.float32)))
    r, v, na, lw, kd, bv, g, bonus = _rwkv_pre(
        pa, seq, row1(rwkv_mu), row1(rwkv_w0), _block_diag2(rwkv_w2), row1(rwkv_a0), _block_diag2(rwkv_a2),
        rwkv_g2, row1(rwkv_k_k), row1(rwkv_k_a), row1(rwkv_r_k), bd, tm)
    y = _scan(r, v, na, lw, kd, bv, batch, seq)

    groups = gmlp_w_s.shape[0]
    wscat = jnp.concatenate([gmlp_w_s[0::2], gmlp_w_s[1::2]], axis=2).astype(BF16)
    bsfull = jnp.repeat(gmlp_b_s.T, HEAD_DIM, axis=1)
    assert groups * HEAD_DIM == width
    h1 = _mixer(h_in, y, g, bonus, pb, pg, row1(lng), row1(lnb), row1(rwkv_gn_g), row1(rwkv_gn_b), bd,
                row1(gmlp_ln_g), row1(gmlp_ln_b), wscat, bsfull, w_branch.astype(BF16), w_mix_out.astype(BF16),
                row1(ln1_g), row1(ln1_b), alpha, chunk, tm)

    kv = _memkv(mem2, row1(mem_ln_g), row1(mem_ln_b), xattn_w_kv.astype(BF16), mem_len)
    xheads = 4
    h2 = _xattn(h1, kv, xattn_w_q.astype(BF16), xattn_w_o.astype(BF16), row1(ln2_g), row1(ln2_b),
                alpha, xheads, seq, mem_len, tm)

    pheads = peer_w_query.shape[1] // (2 * LANES)
    tt = 256
    cnt, e1, rank2, e2 = _route(h2, peer_w_query, peer_sub_keys, pheads, tt)
    h3 = _peer(h2, peer_u.astype(BF16), peer_v.T.astype(BF16), cnt, e1, rank2, e2, row1(ln3_g), row1(ln3_b),
               alpha, pheads, tt, 2048)
    return h3


def kernel(x, mem, ln_emb_g, ln_emb_b, w_in, rwkv_mu, rwkv_w0, rwkv_w2, rwkv_a0, rwkv_a2, rwkv_g2, rwkv_k_k, rwkv_k_a, rwkv_r_k, rwkv_gn_g, rwkv_gn_b, gmlp_ln_g, gmlp_ln_b, gmlp_w_s, gmlp_b_s, w_branch, w_mix_out, ln1_g, ln1_b, mem_ln_g, mem_ln_b, xattn_w_q, xattn_w_kv, xattn_w_o, ln2_g, ln2_b, peer_w_query, peer_sub_keys, peer_u, peer_v, ln3_g, ln3_b):
    batch, seq, d = x.shape
    mem_len = mem.shape[1]
    depth = w_in.shape[0]
    assert depth == 1, "the layer pipeline applies the embedding LN inside the first layer only"
    alpha = float((2.0 * depth) ** 0.25)
    x2 = x.reshape(batch * seq, d)
    mem2 = mem.reshape(batch * mem_len, d)
    flat = lambda t: t.reshape(-1)
    p = (w_in[0], rwkv_mu[0], flat(rwkv_w0[0]), rwkv_w2[0], flat(rwkv_a0[0]), rwkv_a2[0], rwkv_g2[0],
         rwkv_k_k[0], rwkv_k_a[0], flat(rwkv_r_k[0]), rwkv_gn_g[0], rwkv_gn_b[0], gmlp_ln_g[0], gmlp_ln_b[0],
         gmlp_w_s[0], gmlp_b_s[0], w_branch[0], w_mix_out[0], ln1_g[0], ln1_b[0], mem_ln_g[0], mem_ln_b[0],
         xattn_w_q[0], xattn_w_kv[0], xattn_w_o[0], ln2_g[0], ln2_b[0], peer_w_query[0], peer_sub_keys[0],
         peer_u[0], peer_v[0], ln3_g[0], ln3_b[0], alpha)
    h = _layer(x2, mem2, batch, seq, mem_len, ln_emb_g, ln_emb_b, True, p)
    return h.reshape(batch, seq, d)
```

```python
import functools

import numpy as np
import jax
import jax.numpy as jnp
from jax import lax
from jax.experimental import pallas as pl
from jax.experimental.pallas import tpu as pltpu

F32 = jnp.float32
BF16 = jnp.bfloat16
HIGHEST = lax.Precision.HIGHEST

LN_EPS = 1e-5
GN_EPS = 64e-5
LANES = 128
HEAD_DIM = 64
SCAN_CHUNK = 64
TOPK = 16
N_KEYS = 128
VMEM_LIMIT = 56 * 1024 * 1024


def _cparams(*sem):
    return pltpu.CompilerParams(dimension_semantics=sem, vmem_limit_bytes=VMEM_LIMIT)


def _ln(x, g, b, eps=LN_EPS):
    mu = jnp.mean(x, axis=-1, keepdims=True)
    xc = x - mu
    var = jnp.mean(xc * xc, axis=-1, keepdims=True)
    return xc * lax.rsqrt(var + eps) * g + b


def _mm(a, b):
    return jnp.dot(a.astype(BF16), b.astype(BF16), preferred_element_type=F32)


def _mm_nt(a, b):
    return lax.dot_general(a, b, (((1,), (1,)), ((), ())), preferred_element_type=F32)


def _mm_nt_hi(a, b):
    return lax.dot_general(a, b, (((1,), (1,)), ((), ())), precision=HIGHEST,
                           preferred_element_type=F32)


def _split_bf16(x):
    hi = x.astype(BF16)
    return hi, (x - hi.astype(F32)).astype(BF16)


def _hilo(w):
    hi, lo = _split_bf16(w)
    return jnp.stack([hi, lo])


def _mm_x3(x, w_ref):
    hi, lo = _split_bf16(x)
    w_hi = w_ref[0]
    return (jnp.dot(hi, w_hi, preferred_element_type=F32) + jnp.dot(hi, w_ref[1], preferred_element_type=F32)
            + jnp.dot(lo, w_hi, preferred_element_type=F32))


def _segsum(x, ones):
    hi, lo = _split_bf16(x)
    return jnp.dot(hi, ones, preferred_element_type=F32) + jnp.dot(lo, ones, preferred_element_type=F32)


def _gelu(x):
    return 0.5 * x * (1.0 + lax.erf(x * np.float32(1.0 / np.sqrt(2.0))))


def _sigmoid(x):
    return 1.0 / (1.0 + jnp.exp(-x))


def _full(shape):
    nd = len(shape)
    return pl.BlockSpec(shape, lambda *_: (0,) * nd)


def _inproj_kernel(x_ref, g_ref, b_ref, wa_ref, wb_ref, wg_ref, pa_ref, pb_ref, pg_ref):
    h = _ln(x_ref[...], g_ref[...], b_ref[...]).astype(BF16)
    pa_ref[...] = jnp.dot(h, wa_ref[...], preferred_element_type=F32)
    pb_ref[...] = jnp.dot(h, wb_ref[...], preferred_element_type=F32)
    pg_ref[...] = jnp.dot(h, wg_ref[...], preferred_element_type=F32)


def _inproj(x2, g, b, wa, wb, wg, tm):
    n, d = x2.shape
    ca, cb, cg = wa.shape[1], wb.shape[1], wg.shape[1]
    row = lambda c: pl.BlockSpec((tm, c), lambda i: (i, 0))
    return pl.pallas_call(
        _inproj_kernel,
        grid=(n // tm,),
        in_specs=[row(d), _full((1, d)), _full((1, d)), _full((d, ca)), _full((d, cb)), _full((d, cg))],
        out_specs=[row(ca), row(cb), row(cg)],
        out_shape=[jax.ShapeDtypeStruct((n, c), F32) for c in (ca, cb, cg)],
        compiler_params=_cparams("parallel"),
        name="inproj",
    )(x2, g, b, wa, wb, wg)


def _rwkv_pre_kernel(tiles_per_seq, width,
                     p_ref, prev_ref, next_ref, mu_ref, w0_ref, w2_ref, a0_ref, a2_ref, g2_ref,
                     kk_ref, ka_ref, rk_ref, bd_ref,
                     r_out, v_out, na_out, lw_out, kd_out, bv_out, g_out, bonus_out):
    i = pl.program_id(0)
    pos = i % tiles_per_seq
    p = p_ref[...]
    tm = p.shape[0]
    row = lax.broadcasted_iota(jnp.int32, (tm, 1), 0)
    prev_row = jnp.where(pos == 0, 0.0, prev_ref[7:8, :])
    next_row = jnp.where(pos == tiles_per_seq - 1, 0.0, next_ref[0:1, :])
    prev = jnp.where(row == 0, prev_row, pltpu.roll(p, 1, 0))
    nxt = jnp.where(row == tm - 1, next_row, pltpu.roll(p, tm - 1, 0))
    ps = p + mu_ref[...] * (0.5 * (prev + nxt) - p)

    w = width
    r = ps[:, 0:w]
    k = ps[:, w:2 * w]
    v = ps[:, 2 * w:3 * w]
    wd = ps[:, 3 * w:3 * w + LANES]
    ad = ps[:, 3 * w + LANES:3 * w + 2 * LANES]
    gd = ps[:, 3 * w + 2 * LANES:3 * w + 3 * LANES]

    wz = w0_ref[...] + _mm_x3(jnp.tanh(wd), w2_ref)
    w_log = -(jnp.maximum(-wz, 0.0) + jnp.log(1.0 + jnp.exp(-jnp.abs(wz)))) - 0.5
    log_decay = -jnp.exp(w_log)
    a = _sigmoid(a0_ref[...] + _mm_x3(ad, a2_ref))
    g = _mm_x3(_sigmoid(gd), g2_ref)

    bd = bd_ref[...]
    kk = k * kk_ref[...]
    nrm = jnp.sqrt(_segsum(kk * kk, bd))
    kk = kk / jnp.maximum(nrm, 1e-12)

    ka = ka_ref[...]
    k0 = k * (1.0 + (a[:, 0:w] - 1.0) * ka)
    k1 = k * (1.0 + (a[:, w:2 * w] - 1.0) * ka)
    rk = _segsum(r * (k0 + k1) * rk_ref[...], bd)

    r_out[...] = r
    v_out[...] = v
    na_out[...] = -kk
    lw_out[0] = log_decay[:, 0:w]
    lw_out[1] = log_decay[:, w:2 * w]
    kd_out[0] = k0
    kd_out[1] = k1
    bv_out[0] = kk * a[:, 0:w]
    bv_out[1] = kk * a[:, w:2 * w]
    g_out[...] = g
    bonus_out[...] = rk * v


def _rwkv_pre(pa, seq, mu, w0, w2bd, a0, a2bd, g2, k_k, k_a, r_k, bd, tm):
    n, ca = pa.shape
    w = bd.shape[0]
    tps = seq // tm
    hb = tm // 8
    nb8 = n // 8
    row = lambda c: pl.BlockSpec((tm, c), lambda i: (i, 0))
    row2 = pl.BlockSpec((2, tm, w), lambda i: (0, i, 0))
    outs = [jax.ShapeDtypeStruct((n, w), F32)] * 3 + [jax.ShapeDtypeStruct((2, n, w), F32)] * 3 + \
           [jax.ShapeDtypeStruct((n, w), F32)] * 2
    return pl.pallas_call(
        functools.partial(_rwkv_pre_kernel, tps, w),
        grid=(n // tm,),
        in_specs=[row(ca),
                  pl.BlockSpec((8, ca), lambda i: (jnp.maximum(i * hb - 1, 0), 0)),
                  pl.BlockSpec((8, ca), lambda i: (jnp.minimum((i + 1) * hb, nb8 - 1), 0)),
                  _full((1, ca)), _full((1, 2 * w)), _full(w2bd.shape), _full((1, 2 * w)),
                  _full(a2bd.shape), _full(g2.shape), _full((1, w)), _full((1, w)), _full((1, w)),
                  _full((w, w))],
        out_specs=[row(w), row(w), row(w), row2, row2, row2, row(w), row(w)],
        out_shape=outs,
        compiler_params=_cparams("parallel"),
        name="rwkv_pre",
    )(pa, pa, pa, mu, w0, w2bd, a0, a2bd, g2, k_k, k_a, r_k, bd)


def _scan_kernel(rf_ref, vf_ref, af_ref, rb_ref, vb_ref, ab_ref, lwf_ref, kf_ref, bf_ref, lwb_ref, kb_ref, bb_ref,
                 ms_ref, mi_ref, lc_ref, yf_ref, yb_ref, s_scr):
    c = SCAN_CHUNK
    h2 = 2 * c

    @pl.when(pl.program_id(1) == 0)
    def _():
        s_scr[...] = jnp.zeros_like(s_scr)

    first_head = lax.broadcasted_iota(jnp.int32, (1, LANES), 1) < HEAD_DIM

    def stack(x):
        return jnp.concatenate([jnp.where(first_head, x, 0.0), jnp.where(first_head, 0.0, x)], axis=0)

    dot = functools.partial(jnp.dot, preferred_element_type=F32)
    n_groups = lwf_ref.shape[1] // LANES
    sls = [slice(m * LANES, (m + 1) * LANES) for m in range(n_groups)]

    ar, bk, v_s, bk_e, p_end, strict, incl = [], [], [], [], [], [], []
    dirs = ((rf_ref, vf_ref, af_ref, lwf_ref, kf_ref, bf_ref), (rb_ref, vb_ref, ab_ref, lwb_ref, kb_ref, bb_ref))
    for d, (r_ref, v_ref, a_ref, lw_ref, k_ref, b_ref) in enumerate(dirs):
        lw = lw_ref[...]
        l1 = lw.astype(BF16)
        rest = lw - l1.astype(F32)
        l2 = rest.astype(BF16)
        l3 = (rest - l2.astype(F32)).astype(BF16)
        lc = lc_ref[d]
        cum = dot(lc, l1) + dot(lc, l2) + dot(lc, l3)
        tot = jnp.sum(lw, axis=0, keepdims=True)
        e_neg = jnp.exp(-cum)
        e_end = jnp.exp(tot - cum)
        a_t = a_ref[...] * jnp.exp(cum - lw)
        r_t = r_ref[...] * jnp.exp(cum)
        b_in = b_ref[...]
        k_in = k_ref[...]
        b_t = b_in * e_neg
        k_t = k_in * e_neg
        b_e = b_in * e_end
        k_e = k_in * e_end
        v_in = v_ref[...]
        pe = jnp.exp(tot)
        sd = ms_ref[d] > 0.0
        ic = mi_ref[d] > 0.0
        for s in sls:
            ar.append(jnp.concatenate([stack(a_t[:, s]), stack(r_t[:, s])], axis=0).astype(BF16))
            bk.append(jnp.concatenate([stack(b_t[:, s]), stack(k_t[:, s])], axis=0).astype(BF16))
            bk_e.append(jnp.concatenate([stack(b_e[:, s]), stack(k_e[:, s])], axis=0).astype(BF16))
            v_s.append(stack(v_in[:, s]))
            p_end.append(pe[:, s])
            strict.append(sd)
            incl.append(ic)
    gs = range(2 * n_groups)
    v_b = [x.astype(BF16) for x in v_s]
    state = [s_scr[m] for m in gs]

    l_all = [_mm_nt(ar[m], bk[m]) for m in gs]
    from_state = [_mm_nt(ar[m], state[m].astype(BF16)) for m in gs]
    l_ab = [jnp.where(strict[m], l_all[m][:h2, :h2], 0.0).astype(BF16) for m in gs]
    l_k = [jnp.concatenate([jnp.where(strict[m], l_all[m][:h2, h2:], 0.0),
                            jnp.where(incl[m], l_all[m][h2:, h2:], 0.0)], axis=0).astype(BF16) for m in gs]
    l_rb = [jnp.where(incl[m], l_all[m][h2:, :h2], 0.0).astype(BF16) for m in gs]
    from_v = [dot(l_k[m], v_b[m]) for m in gs]

    x = [from_state[m][:h2] + from_v[m][:h2] for m in gs]
    lp = l_ab
    x = [x[m] + dot(lp[m], x[m].astype(BF16)) for m in gs]
    span = 2
    while span < c:
        lp = [dot(lp[m], lp[m]).astype(BF16) for m in gs]
        x = [x[m] + dot(lp[m], x[m].astype(BF16)) for m in gs]
        span *= 2
    u_s = x

    y_s = [from_state[m][h2:] + from_v[m][h2:] + dot(l_rb[m], u_s[m].astype(BF16)) for m in gs]
    uv_t = [jnp.transpose(jnp.concatenate([u_s[m], v_s[m]], axis=0)).astype(BF16) for m in gs]
    new_state = [state[m] * p_end[m] + dot(uv_t[m], bk_e[m]) for m in gs]
    for m in gs:
        y_ref = yf_ref if m < n_groups else yb_ref
        y_ref[:, sls[m % n_groups]] = y_s[m][:c] + y_s[m][c:]
        s_scr[m] = new_state[m]


def _scan(r, v, na, lw, kd, bv, batch, seq):
    n, w = r.shape
    c = SCAN_CHUNK
    nc = seq // c
    t = np.arange(c)
    before = [(t[None, :] < t[:, None]), (t[None, :] > t[:, None])]
    eye = np.eye(c, dtype=bool)
    tile2 = lambda m: np.tile(m, (2, 2))
    ms = jnp.asarray(np.stack([tile2(m) for m in before]).astype(np.float32))
    mi = jnp.asarray(np.stack([tile2(m | eye) for m in before]).astype(np.float32))
    lc = jnp.asarray(np.stack([(m | eye) for m in before]).astype(np.float32), BF16)

    fwd = pl.BlockSpec((c, w), lambda b, ci: (b * nc + ci, 0))
    bwd = pl.BlockSpec((c, w), lambda b, ci: (b * nc + nc - 1 - ci, 0))
    fwd_d = pl.BlockSpec((None, c, w), lambda b, ci: (0, b * nc + ci, 0))
    bwd_d = pl.BlockSpec((None, c, w), lambda b, ci: (1, b * nc + nc - 1 - ci, 0))
    return pl.pallas_call(
        _scan_kernel,
        grid=(batch, nc),
        in_specs=[fwd, fwd, fwd, bwd, bwd, bwd, fwd_d, fwd_d, fwd_d, bwd_d, bwd_d, bwd_d,
                  _full(ms.shape), _full(mi.shape), _full(lc.shape)],
        out_specs=[fwd, bwd],
        out_shape=[jax.ShapeDtypeStruct((n, w), F32)] * 2,
        scratch_shapes=[pltpu.VMEM((2 * (w // LANES), LANES, LANES), F32)],
        compiler_params=_cparams("parallel", "arbitrary"),
        name="scan",
    )(r, v, na, r, v, na, lw, kd, bv, lw, kd, bv, ms, mi, lc)


def _mixer_kernel(alpha, chunk,
                  x_ref, yf_ref, yb_ref, g_ref, bonus_ref, pb_ref, pg_ref,
                  lng_ref, lnb_ref, gng_ref, gnb_ref, bd_ref, mlg_ref, mlb_ref, ws_ref, bs_ref,
                  wbr_ref, wmix_ref, l1g_ref, l1b_ref, o_ref):
    h0 = _ln(x_ref[...], lng_ref[...], lnb_ref[...])
    bd = bd_ref[...]
    inv = np.float32(1.0 / HEAD_DIM)

    y = yf_ref[...] + yb_ref[...]
    mu = _segsum(y, bd) * inv
    yc = y - mu
    var = _segsum(yc * yc, bd) * inv
    y_a = (yc * lax.rsqrt(var + GN_EPS) * gng_ref[...] + gnb_ref[...] + bonus_ref[...]) * g_ref[...]

    gp = _gelu(pb_ref[...])
    w = gp.shape[1] // 2
    u = gp[:, :w]
    vn = _ln(gp[:, w:], mlg_ref[...], mlb_ref[...])
    tm = u.shape[0]
    first_group = lax.broadcasted_iota(jnp.int32, (1, LANES), 1) < HEAD_DIM
    rows = []
    for ci in range(tm // chunk):
        vc = vn[ci * chunk:(ci + 1) * chunk]
        cols = []
        for m in range(w // LANES):
            vp = vc[:, m * LANES:(m + 1) * LANES]
            stacked = jnp.concatenate([jnp.where(first_group, vp, 0.0), jnp.where(first_group, 0.0, vp)], axis=0)
            cols.append(_mm(ws_ref[m], stacked))
        rows.append(jnp.concatenate(cols, axis=1) + bs_ref[...])
    sv = jnp.concatenate(rows, axis=0)
    y_b = u * sv

    pg = pg_ref[...]
    d = pg.shape[1] // 2
    merged = _sigmoid(pg[:, :d]) * _mm(y_a, wbr_ref[0]) + _sigmoid(pg[:, d:]) * _mm(y_b, wbr_ref[1])
    mix = _mm(merged, wmix_ref[...])
    o_ref[...] = _ln(alpha * h0 + mix, l1g_ref[...], l1b_ref[...])


def _mixer(x2, yf, yb, g, bonus, pb, pg, lng, lnb, gng, gnb, bd, mlg, mlb, wscat, bsfull, wbr, wmix, l1g, l1b,
           alpha, chunk, tm):
    n, d = x2.shape
    w = g.shape[1]
    row = lambda c: pl.BlockSpec((tm, c), lambda i: (i, 0))
    return pl.pallas_call(
        functools.partial(_mixer_kernel, alpha, chunk),
        grid=(n // tm,),
        in_specs=[row(d), row(w), row(w), row(w), row(w), row(2 * w), row(2 * d),
                  _full((1, d)), _full((1, d)), _full((1, w)), _full((1, w)), _full((w, w)),
                  _full((1, w)), _full((1, w)), _full(wscat.shape), _full(bsfull.shape),
                  _full(wbr.shape), _full(wmix.shape), _full((1, d)), _full((1, d))],
        out_specs=row(d),
        out_shape=jax.ShapeDtypeStruct((n, d), F32),
        compiler_params=_cparams("parallel"),
        name="mixer",
    )(x2, yf, yb, g, bonus, pb, pg, lng, lnb, gng, gnb, bd, mlg, mlb, wscat, bsfull, wbr, wmix, l1g, l1b)


def _memkv_kernel(m_ref, g_ref, b_ref, w_ref, o_ref):
    o_ref[...] = _mm(_ln(m_ref[...], g_ref[...], b_ref[...]), w_ref[...]).astype(BF16)


def _memkv(mem2, g, b, wkv, tm):
    n, d = mem2.shape
    return pl.pallas_call(
        _memkv_kernel,
        grid=(n // tm,),
        in_specs=[pl.BlockSpec((tm, d), lambda i: (i, 0)), _full((1, d)), _full((1, d)), _full(wkv.shape)],
        out_specs=pl.BlockSpec((tm, wkv.shape[1]), lambda i: (i, 0)),
        out_shape=jax.ShapeDtypeStruct((n, wkv.shape[1]), BF16),
        compiler_params=_cparams("parallel"),
        name="memkv",
    )(mem2, g, b, wkv)


def _xattn_kernel(alpha, heads, h_ref, kv_ref, wq_ref, wo_ref, g_ref, b_ref, o_ref):
    h = h_ref[...]
    d = h.shape[1]
    hd = d // heads
    q = _mm(h, wq_ref[...])
    kv = kv_ref[...]
    scale = np.float32(hd ** -0.5)
    out = jnp.zeros_like(h)
    for i in range(heads):
        qh = q[:, i * hd:(i + 1) * hd].astype(BF16)
        kh = kv[:, i * hd:(i + 1) * hd]
        vh = kv[:, d + i * hd:d + (i + 1) * hd]
        s = lax.dot_general(qh, kh, (((1,), (1,)), ((), ())), preferred_element_type=F32) * scale
        s = s - jnp.max(s, axis=-1, keepdims=True)
        e = jnp.exp(s)
        prob = e / jnp.sum(e, axis=-1, keepdims=True)
        oh = _mm(prob, vh)
        out = out + _mm(oh, wo_ref[i * hd:(i + 1) * hd, :])
    o_ref[...] = _ln(alpha * h + out, g_ref[...], b_ref[...])


def _xattn(h1, kv, wq, wo, g, b, alpha, heads, seq, mem_len, tm):
    n, d = h1.shape
    tps = seq // tm
    return pl.pallas_call(
        functools.partial(_xattn_kernel, alpha, heads),
        grid=(n // tm,),
        in_specs=[pl.BlockSpec((tm, d), lambda i: (i, 0)),
                  pl.BlockSpec((mem_len, 2 * d), lambda i: (i // tps, 0)),
                  _full(wq.shape), _full(wo.shape), _full((1, d)), _full((1, d))],
        out_specs=pl.BlockSpec((tm, d), lambda i: (i, 0)),
        out_shape=jax.ShapeDtypeStruct((n, d), F32),
        compiler_params=_cparams("parallel"),
        name="xattn",
    )(h1, kv, wq, wo, g, b)


def _extract_topk(s, pos, k):
    rank = jnp.full(s.shape, float(k), F32)
    vals = []
    for j in range(k):
        m = jnp.max(s, axis=0, keepdims=True)
        first = jnp.min(jnp.where(s == m, pos, np.float32(np.inf)), axis=0, keepdims=True)
        sel = pos == first
        rank = jnp.where(sel, float(j), rank)
        s = jnp.where(sel, -jnp.inf, s)
        vals.append(m)
    return vals, rank


def _candidate_layout(k):
    pieces = [("row", 0, 0, 16), ("row", 1, 0, 8), ("col", 0, 0, 16), ("col", 1, 0, 8),
              ("col", 2, 0, 8), ("col", 3, 0, 8), ("col", 4, 0, 8)]
    pos, seen = [], set()
    for kind, fixed, start, n in pieces:
        for o in range(start, start + n):
            i, j = (fixed, o) if kind == "row" else (o, fixed)
            ok = (i + 1) * (j + 1) <= k and (i, j) not in seen
            if ok:
                seen.add((i, j))
            pos.append(float(i * k + j) if ok else -1.0)
    assert len(seen) == sum(k // (i + 1) for i in range(k))
    return pieces, np.asarray(pos, np.float32)


def _route_kernel(heads, h_ref, wqh_ref, wql_ref, sk_ref, cpos_ref, cnt_ref, e1_ref, rank2_ref, e2_ref, q_scr):
    h = h_ref[...]
    h_hi = h.astype(BF16)
    h_lo = (h - h_hi.astype(F32)).astype(BF16)
    q_scr[...] = (jnp.dot(h_hi, wqh_ref[...], preferred_element_type=F32)
                  + jnp.dot(h_hi, wql_ref[...], preferred_element_type=F32)
                  + jnp.dot(h_lo, wqh_ref[...], preferred_element_type=F32))
    k = TOPK
    pieces, _ = _candidate_layout(k)
    cpos = cpos_ref[...]
    cvalid = cpos >= 0.0
    key_pos = lax.broadcasted_iota(jnp.int32, (N_KEYS, h.shape[0]), 0).astype(F32)
    row16 = lax.broadcasted_iota(jnp.int32, (k, 1), 0)

    def per_head(hd, carry):
        off = pl.multiple_of(hd * (2 * LANES), 2 * LANES)
        s1 = _mm_nt_hi(sk_ref[0], q_scr[:, pl.ds(off, LANES)])
        s2 = _mm_nt_hi(sk_ref[1], q_scr[:, pl.ds(off + LANES, LANES)])
        top1, rank1 = _extract_topk(s1, key_pos, k)
        top2, rank2 = _extract_topk(s2, key_pos, k)
        t1 = jnp.concatenate(top1, axis=0)
        t2 = jnp.concatenate(top2, axis=0)
        parts = []
        for kind, fixed, start, n in pieces:
            if kind == "row":
                parts.append(top1[fixed] + t2[start:start + n])
            else:
                parts.append(t1[start:start + n] + top2[fixed])
        cand = jnp.where(cvalid, jnp.concatenate(parts, axis=0), -jnp.inf)
        best, crank = _extract_topk(cand, jnp.where(cvalid, cpos, np.float32(np.inf)), k)
        chosen = jnp.where(crank < float(k), 1.0, 0.0)
        n16 = jnp.zeros((k, h.shape[0]), F32)
        r0 = 0
        for kind, fixed, start, n in pieces:
            c = chosen[r0:r0 + n]
            r0 += n
            if kind == "row":
                n16 = n16 + jnp.where(row16 == fixed, jnp.sum(c, axis=0, keepdims=True), 0.0)
            elif n == k:
                n16 = n16 + c
            else:
                n16 = n16 + jnp.concatenate([c, jnp.zeros((k - n, h.shape[0]), F32)], axis=0)
        cnt_dense = jnp.zeros_like(s1)
        for i in range(k):
            cnt_dense = cnt_dense + jnp.where(rank1 == float(i), n16[i:i + 1], 0.0)
        z = jnp.zeros_like(best[0])
        for j in range(k):
            z = z + jnp.exp(best[j] - best[0])
        cnt_ref[hd] = cnt_dense
        e1_ref[hd] = jnp.exp(s1 - top1[0]) / z
        rank2_ref[hd] = rank2.astype(BF16)
        e2_ref[hd] = jnp.exp(s2 - top2[0]).astype(BF16)
        return carry

    lax.fori_loop(0, heads, per_head, 0)


def _route(h2, wq, sk, heads, tt):
    n, d = h2.shape
    wq_hi = wq.astype(BF16)
    wq_lo = (wq - wq_hi.astype(F32)).astype(BF16)
    _, pos = _candidate_layout(TOPK)
    cpos = jnp.asarray(np.broadcast_to(pos[:, None], (pos.shape[0], tt)).copy())
    blk = pl.BlockSpec((heads, N_KEYS, tt), lambda i: (0, 0, i))
    return pl.pallas_call(
        functools.partial(_route_kernel, heads),
        grid=(n // tt,),
        in_specs=[pl.BlockSpec((tt, d), lambda i: (i, 0)), _full(wq.shape), _full(wq.shape), _full(sk.shape),
                  _full(cpos.shape)],
        out_specs=[blk] * 4,
        out_shape=[jax.ShapeDtypeStruct((heads, N_KEYS, n), F32)] * 2
                  + [jax.ShapeDtypeStruct((heads, N_KEYS, n), BF16)] * 2,
        scratch_shapes=[pltpu.VMEM((tt, wq.shape[1]), F32)],
        compiler_params=_cparams("parallel"),
        name="route",
    )(h2, wq_hi, wq_lo, sk, cpos)


def _peer_kernel(alpha, heads, n_pairs, h_ref, ua_ref, ub_ref, vtp_ref, vtc_ref,
                 cnt_ref, e1_ref, rank2_ref, e2_ref, g_ref, b_ref, o_ref, acc_scr, wa_scr, wb_scr, hb_scr):
    p = pl.program_id(1)
    te = ua_ref.shape[0]
    tt = h_ref.shape[0]
    blocks = te // N_KEYS
    sub = 16
    grp = N_KEYS // sub
    dot = functools.partial(jnp.dot, preferred_element_type=F32)

    @pl.when(p == 0)
    def _():
        acc_scr[...] = jnp.zeros_like(acc_scr)
        wb_scr[...] = jnp.zeros_like(wb_scr)
        hb_scr[...] = h_ref[...].astype(BF16)

    def gate_tile(u_ref, tile, w_out):
        hb = hb_scr[...]
        for il in range(blocks):
            ig = tile * blocks + il
            act = lax.dot_general(u_ref[il * N_KEYS:(il + 1) * N_KEYS, :], hb, (((1,), (1,)), ((), ())),
                                  preferred_element_type=F32)
            gate = jnp.zeros((grp, sub, tt), BF16)
            for hd in range(heads):
                cnt = jnp.broadcast_to(cnt_ref[hd, pl.ds(ig, 1), :], (sub, tt)).astype(BF16)
                e1 = jnp.broadcast_to(e1_ref[hd, pl.ds(ig, 1), :], (sub, tt)).astype(BF16)
                rank2 = rank2_ref[hd].reshape(grp, sub, tt)
                e2 = e2_ref[hd].reshape(grp, sub, tt)
                gate = gate + jnp.where(rank2 < cnt[None], e2, jnp.zeros_like(e2)) * e1[None]
            w = gate * _gelu(act).astype(BF16).reshape(grp, sub, tt)
            w_out[il * N_KEYS:(il + 1) * N_KEYS, :] = w.reshape(N_KEYS, tt)

    @pl.when(p < n_pairs)
    def _():
        acc_scr[...] += dot(vtp_ref[...], wb_scr[...])
        gate_tile(ua_ref, 2 * p, wa_scr)
        acc_scr[...] += dot(vtc_ref[...], wa_scr[...])
        gate_tile(ub_ref, 2 * p + 1, wb_scr)

    @pl.when(p == n_pairs)
    def _():
        y = jnp.transpose(acc_scr[...] + dot(vtp_ref[...], wb_scr[...]))
        o_ref[...] = _ln(alpha * h_ref[...] + y, g_ref[...], b_ref[...])


def _peer(h2, u_bf, vt_bf, cnt, e1, rank2, e2, g, b, alpha, heads, tt, te):
    n, d = h2.shape
    ne = u_bf.shape[0] // te
    n_pairs = ne // 2
    assert ne == 2 * n_pairs and ne >= 2
    blk = pl.BlockSpec((heads, N_KEYS, tt), lambda i, p: (0, 0, i))
    return pl.pallas_call(
        functools.partial(_peer_kernel, alpha, heads, n_pairs),
        grid=(n // tt, n_pairs + 1),
        in_specs=[pl.BlockSpec((tt, d), lambda i, p: (i, 0)),
                  pl.BlockSpec((te, d), lambda i, p: (jnp.minimum(2 * p, ne - 2), 0)),
                  pl.BlockSpec((te, d), lambda i, p: (jnp.minimum(2 * p + 1, ne - 1), 0)),
                  pl.BlockSpec((d, te), lambda i, p: (0, jnp.maximum(2 * p - 1, 1))),
                  pl.BlockSpec((d, te), lambda i, p: (0, jnp.minimum(2 * p, ne - 2))),
                  blk, blk, blk, blk, _full((1, d)), _full((1, d))],
        out_specs=pl.BlockSpec((tt, d), lambda i, p: (i, 0)),
        out_shape=jax.ShapeDtypeStruct((n, d), F32),
        scratch_shapes=[pltpu.VMEM((d, tt), F32), pltpu.VMEM((te, tt), BF16), pltpu.VMEM((te, tt), BF16),
                        pltpu.VMEM((tt, d), BF16)],
        compiler_params=_cparams("parallel", "arbitrary"),
        name="peer",
    )(h2, u_bf, u_bf, vt_bf, vt_bf, cnt, e1, rank2, e2, g, b)


def _block_diag2(w):
    z = jnp.zeros_like(w[0])
    return jnp.concatenate([jnp.concatenate([w[0], z], axis=1), jnp.concatenate([z, w[1]], axis=1)], axis=0)


def _layer(h_in, mem2, batch, seq, mem_len, lng, lnb, first, p):
    del first
    n, d = h_in.shape
    (w_in, rwkv_mu, rwkv_w0, rwkv_w2, rwkv_a0, rwkv_a2, rwkv_g2, rwkv_k_k, rwkv_k_a, rwkv_r_k,
     rwkv_gn_g, rwkv_gn_b, gmlp_ln_g, gmlp_ln_b, gmlp_w_s, gmlp_b_s, w_branch, w_mix_out, ln1_g, ln1_b,
     mem_ln_g, mem_ln_b, xattn_w_q, xattn_w_kv, xattn_w_o, ln2_g, ln2_b,
     peer_w_query, peer_sub_keys, peer_u, peer_v, ln3_g, ln3_b, alpha) = p
    width = rwkv_k_k.shape[0]
    rw_cols = rwkv_mu.shape[0]
    gm_cols = 2 * gmlp_ln_g.shape[0]
    chunk = gmlp_w_s.shape[1]
    row1 = lambda t: t.reshape(1, -1)

    tm = 256
    wa = w_in[:, :rw_cols].astype(BF16)
    wb = w_in[:, rw_cols:rw_cols + gm_cols].astype(BF16)
    wg = w_in[:, rw_cols + gm_cols:].astype(BF16)
    pa, pb, pg = _inproj(h_in, row1(lng), row1(lnb), wa, wb, wg, tm)

    heads_r = width // HEAD_DIM
    bd = jnp.asarray(np.kron(np.eye(heads_r, dtype=np.float32), np.ones((HEAD_DIM, HEAD_DIM), np.float32)), BF16)
    r, v, na, lw, kd, bv, g, bonus = _rwkv_pre(
        pa, seq, row1(rwkv_mu), row1(rwkv_w0), _hilo(_block_diag2(rwkv_w2)), row1(rwkv_a0),
        _hilo(_block_diag2(rwkv_a2)), _hilo(rwkv_g2), row1(rwkv_k_k), row1(rwkv_k_a), row1(rwkv_r_k), bd, tm)
    yf, yb = _scan(r, v, na, lw, kd, bv, batch, seq)

    groups = gmlp_w_s.shape[0]
    wscat = jnp.concatenate([gmlp_w_s[0::2], gmlp_w_s[1::2]], axis=2).astype(BF16)
    bsfull = jnp.repeat(gmlp_b_s.T, HEAD_DIM, axis=1)
    assert groups * HEAD_DIM == width
    h1 = _mixer(h_in, yf, yb, g, bonus, pb, pg, row1(lng), row1(lnb), row1(rwkv_gn_g), row1(rwkv_gn_b), bd,
                row1(gmlp_ln_g), row1(gmlp_ln_b), wscat, bsfull, w_branch.astype(BF16), w_mix_out.astype(BF16),
                row1(ln1_g), row1(ln1_b), alpha, chunk, tm)

    kv = _memkv(mem2, row1(mem_ln_g), row1(mem_ln_b), xattn_w_kv.astype(BF16), mem_len)
    xheads = 4
    h2 = _xattn(h1, kv, xattn_w_q.astype(BF16), xattn_w_o.astype(BF16), row1(ln2_g), row1(ln2_b),
                alpha, xheads, seq, mem_len, tm)

    pheads = peer_w_query.shape[1] // (2 * LANES)
    tt = 256
    cnt, e1, rank2, e2 = _route(h2, peer_w_query, peer_sub_keys, pheads, tt)
    h3 = _peer(h2, peer_u.astype(BF16), peer_v.T.astype(BF16), cnt, e1, rank2, e2, row1(ln3_g), row1(ln3_b),
               alpha, pheads, tt, 1024)
    return h3


def kernel(x, mem, ln_emb_g, ln_emb_b, w_in, rwkv_mu, rwkv_w0, rwkv_w2, rwkv_a0, rwkv_a2, rwkv_g2, rwkv_k_k, rwkv_k_a, rwkv_r_k, rwkv_gn_g, rwkv_gn_b, gmlp_ln_g, gmlp_ln_b, gmlp_w_s, gmlp_b_s, w_branch, w_mix_out, ln1_g, ln1_b, mem_ln_g, mem_ln_b, xattn_w_q, xattn_w_kv, xattn_w_o, ln2_g, ln2_b, peer_w_query, peer_sub_keys, peer_u, peer_v, ln3_g, ln3_b):
    batch, seq, d = x.shape
    mem_len = mem.shape[1]
    depth = w_in.shape[0]
    assert depth == 1, "the layer pipeline applies the embedding LN inside the first layer only"
    alpha = float((2.0 * depth) ** 0.25)
    x2 = x.reshape(batch * seq, d)
    mem2 = mem.reshape(batch * mem_len, d)
    flat = lambda t: t.reshape(-1)
    p = (w_in[0], rwkv_mu[0], flat(rwkv_w0[0]), rwkv_w2[0], flat(rwkv_a0[0]), rwkv_a2[0], rwkv_g2[0],
         rwkv_k_k[0], rwkv_k_a[0], flat(rwkv_r_k[0]), rwkv_gn_g[0], rwkv_gn_b[0], gmlp_ln_g[0], gmlp_ln_b[0],
         gmlp_w_s[0], gmlp_b_s[0], w_branch[0], w_mix_out[0], ln1_g[0], ln1_b[0], mem_ln_g[0], mem_ln_b[0],
         xattn_w_q[0], xattn_w_kv[0], xattn_w_o[0], ln2_g[0], ln2_b[0], peer_w_query[0], peer_sub_keys[0],
         peer_u[0], peer_v[0], ln3_g[0], ln3_b[0], alpha)
    h = _layer(x2, mem2, batch, seq, mem_len, ln_emb_g, ln_emb_b, True, p)
    return h.reshape(batch, seq, d)
```

```python
import functools

import numpy as np
import jax
import jax.numpy as jnp
from jax import lax
from jax.experimental import pallas as pl
from jax.experimental.pallas import tpu as pltpu

F32 = jnp.float32
BF16 = jnp.bfloat16
HIGHEST = lax.Precision.HIGHEST

LN_EPS = 1e-5
GN_EPS = 64e-5
LANES = 128
HEAD_DIM = 64
SCAN_CHUNK = 64
TOPK = 16
N_KEYS = 128
VMEM_LIMIT = 56 * 1024 * 1024


def _cparams(*sem):
    return pltpu.CompilerParams(dimension_semantics=sem, vmem_limit_bytes=VMEM_LIMIT)


def _ln(x, g, b, eps=LN_EPS):
    mu = jnp.mean(x, axis=-1, keepdims=True)
    xc = x - mu
    var = jnp.mean(xc * xc, axis=-1, keepdims=True)
    return xc * lax.rsqrt(var + eps) * g + b


def _mm(a, b):
    return jnp.dot(a.astype(BF16), b.astype(BF16), preferred_element_type=F32)


def _mm_nt(a, b):
    return lax.dot_general(a, b, (((1,), (1,)), ((), ())), preferred_element_type=F32)


def _mm_nt_hi(a, b):
    return lax.dot_general(a, b, (((1,), (1,)), ((), ())), precision=HIGHEST,
                           preferred_element_type=F32)


def _split_bf16(x):
    hi = x.astype(BF16)
    return hi, (x - hi.astype(F32)).astype(BF16)


def _hilo(w):
    hi, lo = _split_bf16(w)
    return jnp.stack([hi, lo])


def _mm_x3(x, w_ref):
    hi, lo = _split_bf16(x)
    w_hi = w_ref[0]
    return (jnp.dot(hi, w_hi, preferred_element_type=F32) + jnp.dot(hi, w_ref[1], preferred_element_type=F32)
            + jnp.dot(lo, w_hi, preferred_element_type=F32))


def _segsum(x, ones):
    hi, lo = _split_bf16(x)
    return jnp.dot(hi, ones, preferred_element_type=F32) + jnp.dot(lo, ones, preferred_element_type=F32)


def _gelu(x):
    return 0.5 * x * (1.0 + lax.erf(x * np.float32(1.0 / np.sqrt(2.0))))


def _sigmoid(x):
    return 1.0 / (1.0 + jnp.exp(-x))


def _full(shape):
    nd = len(shape)
    return pl.BlockSpec(shape, lambda *_: (0,) * nd)


def _inproj_kernel(x_ref, g_ref, b_ref, wa_ref, wb_ref, wg_ref, pa_ref, pb_ref, pg_ref):
    h = _ln(x_ref[...], g_ref[...], b_ref[...]).astype(BF16)
    pa_ref[...] = jnp.dot(h, wa_ref[...], preferred_element_type=F32)
    pb_ref[...] = jnp.dot(h, wb_ref[...], preferred_element_type=F32)
    pg_ref[...] = jnp.dot(h, wg_ref[...], preferred_element_type=F32)


def _inproj(x2, g, b, wa, wb, wg, tm):
    n, d = x2.shape
    ca, cb, cg = wa.shape[1], wb.shape[1], wg.shape[1]
    row = lambda c: pl.BlockSpec((tm, c), lambda i: (i, 0))
    return pl.pallas_call(
        _inproj_kernel,
        grid=(n // tm,),
        in_specs=[row(d), _full((1, d)), _full((1, d)), _full((d, ca)), _full((d, cb)), _full((d, cg))],
        out_specs=[row(ca), row(cb), row(cg)],
        out_shape=[jax.ShapeDtypeStruct((n, c), F32) for c in (ca, cb, cg)],
        compiler_params=_cparams("parallel"),
        name="inproj",
    )(x2, g, b, wa, wb, wg)


def _rwkv_pre_kernel(tiles_per_seq, width,
                     p_ref, prev_ref, next_ref, mu_ref, w0_ref, w2_ref, a0_ref, a2_ref, g2_ref,
                     kk_ref, ka_ref, rk_ref, bd_ref,
                     r_out, v_out, na_out, lw_out, kd_out, bv_out, g_out, bonus_out):
    i = pl.program_id(0)
    pos = i % tiles_per_seq
    p = p_ref[...]
    tm = p.shape[0]
    row = lax.broadcasted_iota(jnp.int32, (tm, 1), 0)
    prev_row = jnp.where(pos == 0, 0.0, prev_ref[7:8, :])
    next_row = jnp.where(pos == tiles_per_seq - 1, 0.0, next_ref[0:1, :])
    prev = jnp.where(row == 0, prev_row, pltpu.roll(p, 1, 0))
    nxt = jnp.where(row == tm - 1, next_row, pltpu.roll(p, tm - 1, 0))
    ps = p + mu_ref[...] * (0.5 * (prev + nxt) - p)

    w = width
    r = ps[:, 0:w]
    k = ps[:, w:2 * w]
    v = ps[:, 2 * w:3 * w]
    wd = ps[:, 3 * w:3 * w + LANES]
    ad = ps[:, 3 * w + LANES:3 * w + 2 * LANES]
    gd = ps[:, 3 * w + 2 * LANES:3 * w + 3 * LANES]

    wz = w0_ref[...] + _mm_x3(jnp.tanh(wd), w2_ref)
    w_log = -(jnp.maximum(-wz, 0.0) + jnp.log(1.0 + jnp.exp(-jnp.abs(wz)))) - 0.5
    log_decay = -jnp.exp(w_log)
    a = _sigmoid(a0_ref[...] + _mm_x3(ad, a2_ref))
    g = _mm_x3(_sigmoid(gd), g2_ref)

    bd = bd_ref[...]
    kk = k * kk_ref[...]
    nrm = jnp.sqrt(_segsum(kk * kk, bd))
    kk = kk / jnp.maximum(nrm, 1e-12)

    ka = ka_ref[...]
    k0 = k * (1.0 + (a[:, 0:w] - 1.0) * ka)
    k1 = k * (1.0 + (a[:, w:2 * w] - 1.0) * ka)
    rk = _segsum(r * (k0 + k1) * rk_ref[...], bd)

    r_out[...] = r
    v_out[...] = v
    na_out[...] = -kk
    lw_out[0] = log_decay[:, 0:w]
    lw_out[1] = log_decay[:, w:2 * w]
    kd_out[0] = k0
    kd_out[1] = k1
    bv_out[0] = kk * a[:, 0:w]
    bv_out[1] = kk * a[:, w:2 * w]
    g_out[...] = g
    bonus_out[...] = rk * v


def _rwkv_pre(pa, seq, mu, w0, w2bd, a0, a2bd, g2, k_k, k_a, r_k, bd, tm):
    n, ca = pa.shape
    w = bd.shape[0]
    tps = seq // tm
    hb = tm // 8
    nb8 = n // 8
    row = lambda c: pl.BlockSpec((tm, c), lambda i: (i, 0))
    row2 = pl.BlockSpec((2, tm, w), lambda i: (0, i, 0))
    outs = [jax.ShapeDtypeStruct((n, w), F32)] * 3 + [jax.ShapeDtypeStruct((2, n, w), F32)] * 3 + \
           [jax.ShapeDtypeStruct((n, w), F32)] * 2
    return pl.pallas_call(
        functools.partial(_rwkv_pre_kernel, tps, w),
        grid=(n // tm,),
        in_specs=[row(ca),
                  pl.BlockSpec((8, ca), lambda i: (jnp.maximum(i * hb - 1, 0), 0)),
                  pl.BlockSpec((8, ca), lambda i: (jnp.minimum((i + 1) * hb, nb8 - 1), 0)),
                  _full((1, ca)), _full((1, 2 * w)), _full(w2bd.shape), _full((1, 2 * w)),
                  _full(a2bd.shape), _full(g2.shape), _full((1, w)), _full((1, w)), _full((1, w)),
                  _full((w, w))],
        out_specs=[row(w), row(w), row(w), row2, row2, row2, row(w), row(w)],
        out_shape=outs,
        compiler_params=_cparams("parallel"),
        name="rwkv_pre",
    )(pa, pa, pa, mu, w0, w2bd, a0, a2bd, g2, k_k, k_a, r_k, bd)


def _scan_kernel(rf_ref, vf_ref, af_ref, rb_ref, vb_ref, ab_ref, lwf_ref, kf_ref, bf_ref, lwb_ref, kb_ref, bb_ref,
                 ms_ref, mi_ref, lc_ref, yf_ref, yb_ref, s_scr):
    c = SCAN_CHUNK
    h2 = 2 * c

    @pl.when(pl.program_id(1) == 0)
    def _():
        s_scr[...] = jnp.zeros_like(s_scr)

    first_head = lax.broadcasted_iota(jnp.int32, (1, LANES), 1) < HEAD_DIM

    def stack(x):
        return jnp.concatenate([jnp.where(first_head, x, 0.0), jnp.where(first_head, 0.0, x)], axis=0)

    dot = functools.partial(jnp.dot, preferred_element_type=F32)
    n_groups = lwf_ref.shape[1] // LANES
    sls = [slice(m * LANES, (m + 1) * LANES) for m in range(n_groups)]

    ar, bk, v_s, bk_e, p_end, strict, incl = [], [], [], [], [], [], []
    dirs = ((rf_ref, vf_ref, af_ref, lwf_ref, kf_ref, bf_ref), (rb_ref, vb_ref, ab_ref, lwb_ref, kb_ref, bb_ref))
    for d, (r_ref, v_ref, a_ref, lw_ref, k_ref, b_ref) in enumerate(dirs):
        lw = lw_ref[...]
        l1 = lw.astype(BF16)
        rest = lw - l1.astype(F32)
        l2 = rest.astype(BF16)
        l3 = (rest - l2.astype(F32)).astype(BF16)
        lc = lc_ref[d]
        cum = dot(lc, l1) + dot(lc, l2) + dot(lc, l3)
        tot = jnp.sum(lw, axis=0, keepdims=True)
        e_neg = jnp.exp(-cum)
        e_end = jnp.exp(tot - cum)
        a_t = a_ref[...] * jnp.exp(cum - lw)
        r_t = r_ref[...] * jnp.exp(cum)
        b_in = b_ref[...]
        k_in = k_ref[...]
        b_t = b_in * e_neg
        k_t = k_in * e_neg
        b_e = b_in * e_end
        k_e = k_in * e_end
        v_in = v_ref[...]
        pe = jnp.exp(tot)
        sd = ms_ref[d] > 0.0
        ic = mi_ref[d] > 0.0
        for s in sls:
            ar.append(jnp.concatenate([stack(a_t[:, s]), stack(r_t[:, s])], axis=0).astype(BF16))
            bk.append(jnp.concatenate([stack(b_t[:, s]), stack(k_t[:, s])], axis=0).astype(BF16))
            bk_e.append(jnp.concatenate([stack(b_e[:, s]), stack(k_e[:, s])], axis=0).astype(BF16))
            v_s.append(stack(v_in[:, s]))
            p_end.append(pe[:, s])
            strict.append(sd)
            incl.append(ic)
    gs = range(2 * n_groups)
    v_b = [x.astype(BF16) for x in v_s]
    state = [s_scr[m] for m in gs]

    l_all = [_mm_nt(ar[m], bk[m]) for m in gs]
    from_state = [_mm_nt(ar[m], state[m].astype(BF16)) for m in gs]
    l_ab = [jnp.where(strict[m], l_all[m][:h2, :h2], 0.0).astype(BF16) for m in gs]
    l_k = [jnp.concatenate([jnp.where(strict[m], l_all[m][:h2, h2:], 0.0),
                            jnp.where(incl[m], l_all[m][h2:, h2:], 0.0)], axis=0).astype(BF16) for m in gs]
    l_rb = [jnp.where(incl[m], l_all[m][h2:, :h2], 0.0).astype(BF16) for m in gs]
    from_v = [dot(l_k[m], v_b[m]) for m in gs]

    x = [from_state[m][:h2] + from_v[m][:h2] for m in gs]
    lp = l_ab
    x = [x[m] + dot(lp[m], x[m].astype(BF16)) for m in gs]
    span = 2
    while span < c:
        lp = [dot(lp[m], lp[m]).astype(BF16) for m in gs]
        x = [x[m] + dot(lp[m], x[m].astype(BF16)) for m in gs]
        span *= 2
    u_s = x

    y_s = [from_state[m][h2:] + from_v[m][h2:] + dot(l_rb[m], u_s[m].astype(BF16)) for m in gs]
    uv_t = [jnp.transpose(jnp.concatenate([u_s[m], v_s[m]], axis=0)).astype(BF16) for m in gs]
    new_state = [state[m] * p_end[m] + dot(uv_t[m], bk_e[m]) for m in gs]
    for m in gs:
        y_ref = yf_ref if m < n_groups else yb_ref
        y_ref[:, sls[m % n_groups]] = y_s[m][:c] + y_s[m][c:]
        s_scr[m] = new_state[m]


def _scan(r, v, na, lw, kd, bv, batch, seq):
    n, w = r.shape
    c = SCAN_CHUNK
    nc = seq // c
    t = np.arange(c)
    before = [(t[None, :] < t[:, None]), (t[None, :] > t[:, None])]
    eye = np.eye(c, dtype=bool)
    tile2 = lambda m: np.tile(m, (2, 2))
    ms = jnp.asarray(np.stack([tile2(m) for m in before]).astype(np.float32))
    mi = jnp.asarray(np.stack([tile2(m | eye) for m in before]).astype(np.float32))
    lc = jnp.asarray(np.stack([(m | eye) for m in before]).astype(np.float32), BF16)

    fwd = pl.BlockSpec((c, w), lambda b, ci: (b * nc + ci, 0))
    bwd = pl.BlockSpec((c, w), lambda b, ci: (b * nc + nc - 1 - ci, 0))
    fwd_d = pl.BlockSpec((None, c, w), lambda b, ci: (0, b * nc + ci, 0))
    bwd_d = pl.BlockSpec((None, c, w), lambda b, ci: (1, b * nc + nc - 1 - ci, 0))
    return pl.pallas_call(
        _scan_kernel,
        grid=(batch, nc),
        in_specs=[fwd, fwd, fwd, bwd, bwd, bwd, fwd_d, fwd_d, fwd_d, bwd_d, bwd_d, bwd_d,
                  _full(ms.shape), _full(mi.shape), _full(lc.shape)],
        out_specs=[fwd, bwd],
        out_shape=[jax.ShapeDtypeStruct((n, w), F32)] * 2,
        scratch_shapes=[pltpu.VMEM((2 * (w // LANES), LANES, LANES), F32)],
        compiler_params=_cparams("parallel", "arbitrary"),
        name="scan",
    )(r, v, na, r, v, na, lw, kd, bv, lw, kd, bv, ms, mi, lc)


def _mixer_kernel(alpha, chunk,
                  x_ref, yf_ref, yb_ref, g_ref, bonus_ref, pb_ref, pg_ref,
                  lng_ref, lnb_ref, gng_ref, gnb_ref, bd_ref, mlg_ref, mlb_ref, ws_ref, bs_ref,
                  wbr_ref, wmix_ref, l1g_ref, l1b_ref, o_ref):
    h0 = _ln(x_ref[...], lng_ref[...], lnb_ref[...])
    bd = bd_ref[...]
    inv = np.float32(1.0 / HEAD_DIM)

    y = yf_ref[...] + yb_ref[...]
    mu = _segsum(y, bd) * inv
    yc = y - mu
    var = _segsum(yc * yc, bd) * inv
    y_a = (yc * lax.rsqrt(var + GN_EPS) * gng_ref[...] + gnb_ref[...] + bonus_ref[...]) * g_ref[...]

    gp = _gelu(pb_ref[...])
    w = gp.shape[1] // 2
    u = gp[:, :w]
    vn = _ln(gp[:, w:], mlg_ref[...], mlb_ref[...])
    tm = u.shape[0]
    first_group = lax.broadcasted_iota(jnp.int32, (1, LANES), 1) < HEAD_DIM
    rows = []
    for ci in range(tm // chunk):
        vc = vn[ci * chunk:(ci + 1) * chunk]
        cols = []
        for m in range(w // LANES):
            vp = vc[:, m * LANES:(m + 1) * LANES]
            stacked = jnp.concatenate([jnp.where(first_group, vp, 0.0), jnp.where(first_group, 0.0, vp)], axis=0)
            cols.append(_mm(ws_ref[m], stacked))
        rows.append(jnp.concatenate(cols, axis=1) + bs_ref[...])
    sv = jnp.concatenate(rows, axis=0)
    y_b = u * sv

    pg = pg_ref[...]
    d = pg.shape[1] // 2
    merged = _sigmoid(pg[:, :d]) * _mm(y_a, wbr_ref[0]) + _sigmoid(pg[:, d:]) * _mm(y_b, wbr_ref[1])
    mix = _mm(merged, wmix_ref[...])
    o_ref[...] = _ln(alpha * h0 + mix, l1g_ref[...], l1b_ref[...])


def _mixer(x2, yf, yb, g, bonus, pb, pg, lng, lnb, gng, gnb, bd, mlg, mlb, wscat, bsfull, wbr, wmix, l1g, l1b,
           alpha, chunk, tm):
    n, d = x2.shape
    w = g.shape[1]
    row = lambda c: pl.BlockSpec((tm, c), lambda i: (i, 0))
    return pl.pallas_call(
        functools.partial(_mixer_kernel, alpha, chunk),
        grid=(n // tm,),
        in_specs=[row(d), row(w), row(w), row(w), row(w), row(2 * w), row(2 * d),
                  _full((1, d)), _full((1, d)), _full((1, w)), _full((1, w)), _full((w, w)),
                  _full((1, w)), _full((1, w)), _full(wscat.shape), _full(bsfull.shape),
                  _full(wbr.shape), _full(wmix.shape), _full((1, d)), _full((1, d))],
        out_specs=row(d),
        out_shape=jax.ShapeDtypeStruct((n, d), F32),
        compiler_params=_cparams("parallel"),
        name="mixer",
    )(x2, yf, yb, g, bonus, pb, pg, lng, lnb, gng, gnb, bd, mlg, mlb, wscat, bsfull, wbr, wmix, l1g, l1b)


def _memkv_kernel(m_ref, g_ref, b_ref, w_ref, o_ref):
    o_ref[...] = _mm(_ln(m_ref[...], g_ref[...], b_ref[...]), w_ref[...]).astype(BF16)


def _memkv(mem2, g, b, wkv, tm):
    n, d = mem2.shape
    return pl.pallas_call(
        _memkv_kernel,
        grid=(n // tm,),
        in_specs=[pl.BlockSpec((tm, d), lambda i: (i, 0)), _full((1, d)), _full((1, d)), _full(wkv.shape)],
        out_specs=pl.BlockSpec((tm, wkv.shape[1]), lambda i: (i, 0)),
        out_shape=jax.ShapeDtypeStruct((n, wkv.shape[1]), BF16),
        compiler_params=_cparams("parallel"),
        name="memkv",
    )(mem2, g, b, wkv)


def _xattn_kernel(alpha, heads, h_ref, kv_ref, wq_ref, wo_ref, g_ref, b_ref, o_ref):
    h = h_ref[...]
    d = h.shape[1]
    hd = d // heads
    q = _mm(h, wq_ref[...])
    kv = kv_ref[...]
    scale = np.float32(hd ** -0.5)
    out = jnp.zeros_like(h)
    for i in range(heads):
        qh = q[:, i * hd:(i + 1) * hd].astype(BF16)
        kh = kv[:, i * hd:(i + 1) * hd]
        vh = kv[:, d + i * hd:d + (i + 1) * hd]
        s = lax.dot_general(qh, kh, (((1,), (1,)), ((), ())), preferred_element_type=F32) * scale
        s = s - jnp.max(s, axis=-1, keepdims=True)
        e = jnp.exp(s)
        prob = e / jnp.sum(e, axis=-1, keepdims=True)
        oh = _mm(prob, vh)
        out = out + _mm(oh, wo_ref[i * hd:(i + 1) * hd, :])
    o_ref[...] = _ln(alpha * h + out, g_ref[...], b_ref[...])


def _xattn(h1, kv, wq, wo, g, b, alpha, heads, seq, mem_len, tm):
    n, d = h1.shape
    tps = seq // tm
    return pl.pallas_call(
        functools.partial(_xattn_kernel, alpha, heads),
        grid=(n // tm,),
        in_specs=[pl.BlockSpec((tm, d), lambda i: (i, 0)),
                  pl.BlockSpec((mem_len, 2 * d), lambda i: (i // tps, 0)),
                  _full(wq.shape), _full(wo.shape), _full((1, d)), _full((1, d))],
        out_specs=pl.BlockSpec((tm, d), lambda i: (i, 0)),
        out_shape=jax.ShapeDtypeStruct((n, d), F32),
        compiler_params=_cparams("parallel"),
        name="xattn",
    )(h1, kv, wq, wo, g, b)


def _extract_topk(s, pos, k):
    rank = jnp.full(s.shape, float(k), F32)
    vals = []
    for j in range(k):
        m = jnp.max(s, axis=0, keepdims=True)
        first = jnp.min(jnp.where(s == m, pos, np.float32(np.inf)), axis=0, keepdims=True)
        sel = pos == first
        rank = jnp.where(sel, float(j), rank)
        s = jnp.where(sel, -jnp.inf, s)
        vals.append(m)
    return vals, rank


def _bf16_pair_word(x):
    hi = pltpu.bitcast(x.astype(BF16).astype(F32), jnp.int32) & jnp.int32(-65536)
    return hi | lax.shift_right_logical(hi, 16)


def _candidate_layout(k):
    pieces = [("row", 0, 0, 16), ("row", 1, 0, 8), ("col", 0, 0, 16), ("col", 1, 0, 8),
              ("col", 2, 0, 8), ("col", 3, 0, 8), ("col", 4, 0, 8)]
    pos, seen = [], set()
    for kind, fixed, start, n in pieces:
        for o in range(start, start + n):
            i, j = (fixed, o) if kind == "row" else (o, fixed)
            ok = (i + 1) * (j + 1) <= k and (i, j) not in seen
            if ok:
                seen.add((i, j))
            pos.append(float(i * k + j) if ok else -1.0)
    assert len(seen) == sum(k // (i + 1) for i in range(k))
    return pieces, np.asarray(pos, np.float32)


def _sort16_comparators():
    def merge(lo, hi, r):
        step = r * 2
        if step < hi - lo:
            yield from merge(lo, hi, step)
            yield from merge(lo + r, hi, step)
            yield from ((i, i + r) for i in range(lo + r, hi - r, step))
        else:
            yield (lo, lo + r)

    def sort(lo, hi):
        if hi - lo >= 1:
            mid = lo + (hi - lo) // 2
            yield from sort(lo, mid)
            yield from sort(mid + 1, hi)
            yield from merge(lo, hi, 1)

    return tuple(sort(0, 15))


def _top16_sorted(s):
    def exchange(v, i, j):
        v[i], v[j] = jnp.maximum(v[i], v[j]), jnp.minimum(v[i], v[j])

    v = [s[8 * r:8 * r + 8] for r in range(16)]
    for i, j in _sort16_comparators():
        exchange(v, i, j)
    for shift in (4, 2, 1):
        w = [pltpu.roll(x, shift, 0) for x in v]
        v = [jnp.maximum(v[i], w[15 - i]) for i in range(16)]
        for stride in (8, 4, 2, 1):
            for i in range(16):
                if not i & stride:
                    exchange(v, i, i + stride)
    return v


def _route_kernel(heads, h_ref, wqh_ref, wql_ref, sk_ref, cpos_ref, cnt_ref, e1_ref, rank2_ref, e2_ref, q_scr):
    h = h_ref[...]
    tt = h.shape[0]
    h_hi = h.astype(BF16)
    h_lo = (h - h_hi.astype(F32)).astype(BF16)
    q_scr[...] = (jnp.dot(h_hi, wqh_ref[...], preferred_element_type=F32)
                  + jnp.dot(h_hi, wql_ref[...], preferred_element_type=F32)
                  + jnp.dot(h_lo, wqh_ref[...], preferred_element_type=F32))
    k = TOPK
    pieces, _ = _candidate_layout(k)
    cpos = cpos_ref[...]
    cvalid = cpos >= 0.0
    cpos_inf = jnp.where(cvalid, cpos, np.float32(np.inf))
    key_pos = lax.broadcasted_iota(jnp.int32, (N_KEYS, tt), 0).astype(F32)
    row16 = lax.broadcasted_iota(jnp.int32, (k, 1), 0)

    def scores(hd):
        off = pl.multiple_of(hd * (2 * LANES), 2 * LANES)
        s1 = _mm_nt_hi(sk_ref[0], q_scr[:, pl.ds(off, LANES)])
        s2 = _mm_nt_hi(sk_ref[1], q_scr[:, pl.ds(off + LANES, LANES)])
        return s1, s2

    def pair_selection(top1, top2):
        t1 = jnp.concatenate(top1, axis=0)
        t2 = jnp.concatenate(top2, axis=0)
        parts = []
        for kind, fixed, start, n in pieces:
            if kind == "row":
                parts.append(top1[fixed] + t2[start:start + n])
            else:
                parts.append(t1[start:start + n] + top2[fixed])
        cand = jnp.where(cvalid, jnp.concatenate(parts, axis=0), -jnp.inf)
        best, crank = _extract_topk(cand, cpos_inf, k)
        chosen = jnp.where(crank < float(k), 1.0, 0.0)
        n16 = jnp.zeros((k, tt), F32)
        r0 = 0
        for kind, fixed, start, n in pieces:
            c = chosen[r0:r0 + n]
            r0 += n
            if kind == "row":
                n16 = n16 + jnp.where(row16 == fixed, jnp.sum(c, axis=0, keepdims=True), 0.0)
            elif n == k:
                n16 = n16 + c
            else:
                n16 = n16 + jnp.concatenate([c, jnp.zeros((k - n, tt), F32)], axis=0)
        return best, n16

    def emit(hd, s1, s2, max1, max2, best, cnt_dense, rank2):
        z = jnp.zeros_like(best[0])
        for j in range(k):
            z = z + jnp.exp(best[j] - best[0])
        cnt_ref[hd] = _bf16_pair_word(cnt_dense)
        e1_ref[hd] = jnp.exp(s1 - max1) / z
        rank2_ref[hd] = rank2.astype(BF16)
        e2_ref[hd] = jnp.exp(s2 - max2).astype(BF16)

    def head_by_sorting(hd, bad):
        s1, s2 = scores(hd)
        v1 = _top16_sorted(s1)
        v2 = _top16_sorted(s2)
        top1 = [x[0:1] for x in v1]
        top2 = [x[0:1] for x in v2]
        best, n16 = pair_selection(top1, top2)
        rank2 = jnp.zeros_like(s2)
        cnt_dense = jnp.zeros_like(s1)
        for i in range(k):
            rank2 = rank2 + jnp.where(top2[i] > s2, 1.0, 0.0)
            cnt_dense = cnt_dense + jnp.where(s1 == top1[i], n16[i:i + 1], 0.0)
        in1 = jnp.sum(jnp.where(s1 >= top1[k - 1], 1.0, 0.0), axis=0, keepdims=True)
        in2 = jnp.sum(jnp.where(rank2 < float(k), 1.0, 0.0), axis=0, keepdims=True)
        ties = jnp.where(in1 == float(k), 0.0, 1.0) + jnp.where(in2 == float(k), 0.0, 1.0)
        for i in range(k - 1):
            ties = ties + jnp.where(top1[i] > top1[i + 1], 0.0, 1.0) + jnp.where(top2[i] > top2[i + 1], 0.0, 1.0)
        emit(hd, s1, s2, top1[0], top2[0], best, cnt_dense, rank2)
        return bad + ties

    def head_exact(hd, carry):
        s1, s2 = scores(hd)
        top1, rank1 = _extract_topk(s1, key_pos, k)
        top2, rank2 = _extract_topk(s2, key_pos, k)
        best, n16 = pair_selection(top1, top2)
        cnt_dense = jnp.zeros_like(s1)
        for i in range(k):
            cnt_dense = cnt_dense + jnp.where(rank1 == float(i), n16[i:i + 1], 0.0)
        emit(hd, s1, s2, top1[0], top2[0], best, cnt_dense, rank2)
        return carry

    bad = lax.fori_loop(0, heads, head_by_sorting, jnp.zeros((1, tt), F32))

    @pl.when(jnp.sum(bad) > 0.0)
    def _():
        lax.fori_loop(0, heads, head_exact, 0)


def _route(h2, wq, sk, heads, tt):
    n, d = h2.shape
    wq_hi = wq.astype(BF16)
    wq_lo = (wq - wq_hi.astype(F32)).astype(BF16)
    _, pos = _candidate_layout(TOPK)
    cpos = jnp.asarray(np.broadcast_to(pos[:, None], (pos.shape[0], tt)).copy())
    blk = pl.BlockSpec((heads, N_KEYS, tt), lambda i: (0, 0, i))
    return pl.pallas_call(
        functools.partial(_route_kernel, heads),
        grid=(n // tt,),
        in_specs=[pl.BlockSpec((tt, d), lambda i: (i, 0)), _full(wq.shape), _full(wq.shape), _full(sk.shape),
                  _full(cpos.shape)],
        out_specs=[blk] * 4,
        out_shape=[jax.ShapeDtypeStruct((heads, N_KEYS, n), jnp.int32), jax.ShapeDtypeStruct((heads, N_KEYS, n), F32)]
                  + [jax.ShapeDtypeStruct((heads, N_KEYS, n), BF16)] * 2,
        scratch_shapes=[pltpu.VMEM((tt, wq.shape[1]), F32)],
        compiler_params=_cparams("parallel"),
        name="route",
    )(h2, wq_hi, wq_lo, sk, cpos)


def _peer_kernel(alpha, heads, n_pairs, h_ref, ua_ref, ub_ref, vtp_ref, vtc_ref,
                 cnt_ref, e1_ref, rank2_ref, e2_ref, g_ref, b_ref, o_ref, acc_scr, wa_scr, wb_scr, hb_scr):
    p = pl.program_id(1)
    te = ua_ref.shape[0]
    tt = h_ref.shape[0]
    blocks = te // N_KEYS
    sub = 16
    grp = N_KEYS // sub
    dot = functools.partial(jnp.dot, preferred_element_type=F32)

    @pl.when(p == 0)
    def _():
        acc_scr[...] = jnp.zeros_like(acc_scr)
        wb_scr[...] = jnp.zeros_like(wb_scr)
        hb_scr[...] = h_ref[...].astype(BF16)

    def gate_tile(u_ref, tile, w_out):
        hb = hb_scr[...]
        for il in range(blocks):
            ig = tile * blocks + il
            act = lax.dot_general(u_ref[il * N_KEYS:(il + 1) * N_KEYS, :], hb, (((1,), (1,)), ((), ())),
                                  preferred_element_type=F32)
            gate = jnp.zeros((grp, sub, tt), BF16)
            for hd in range(heads):
                cnt = pltpu.bitcast(jnp.broadcast_to(cnt_ref[hd, pl.ds(ig, 1), :], (sub // 2, tt)), BF16)
                e1 = jnp.broadcast_to(e1_ref[hd, pl.ds(ig, 1), :], (sub, tt)).astype(BF16)
                rank2 = rank2_ref[hd].reshape(grp, sub, tt)
                e2 = e2_ref[hd].reshape(grp, sub, tt)
                gate = gate + jnp.where(rank2 < cnt[None], e2, jnp.zeros_like(e2)) * e1[None]
            w = gate * _gelu(act).astype(BF16).reshape(grp, sub, tt)
            w_out[il * N_KEYS:(il + 1) * N_KEYS, :] = w.reshape(N_KEYS, tt)

    @pl.when(p < n_pairs)
    def _():
        acc_scr[...] += dot(vtp_ref[...], wb_scr[...])
        gate_tile(ua_ref, 2 * p, wa_scr)
        acc_scr[...] += dot(vtc_ref[...], wa_scr[...])
        gate_tile(ub_ref, 2 * p + 1, wb_scr)

    @pl.when(p == n_pairs)
    def _():
        y = jnp.transpose(acc_scr[...] + dot(vtp_ref[...], wb_scr[...]))
        o_ref[...] = _ln(alpha * h_ref[...] + y, g_ref[...], b_ref[...])


def _peer(h2, u_bf, vt_bf, cnt, e1, rank2, e2, g, b, alpha, heads, tt, te):
    n, d = h2.shape
    ne = u_bf.shape[0] // te
    n_pairs = ne // 2
    assert ne == 2 * n_pairs and ne >= 2
    blk = pl.BlockSpec((heads, N_KEYS, tt), lambda i, p: (0, 0, i))
    return pl.pallas_call(
        functools.partial(_peer_kernel, alpha, heads, n_pairs),
        grid=(n // tt, n_pairs + 1),
        in_specs=[pl.BlockSpec((tt, d), lambda i, p: (i, 0)),
                  pl.BlockSpec((te, d), lambda i, p: (jnp.minimum(2 * p, ne - 2), 0)),
                  pl.BlockSpec((te, d), lambda i, p: (jnp.minimum(2 * p + 1, ne - 1), 0)),
                  pl.BlockSpec((d, te), lambda i, p: (0, jnp.maximum(2 * p - 1, 1))),
                  pl.BlockSpec((d, te), lambda i, p: (0, jnp.minimum(2 * p, ne - 2))),
                  blk, blk, blk, blk, _full((1, d)), _full((1, d))],
        out_specs=pl.BlockSpec((tt, d), lambda i, p: (i, 0)),
        out_shape=jax.ShapeDtypeStruct((n, d), F32),
        scratch_shapes=[pltpu.VMEM((d, tt), F32), pltpu.VMEM((te, tt), BF16), pltpu.VMEM((te, tt), BF16),
                        pltpu.VMEM((tt, d), BF16)],
        compiler_params=_cparams("parallel", "arbitrary"),
        name="peer",
    )(h2, u_bf, u_bf, vt_bf, vt_bf, cnt, e1, rank2, e2, g, b)


def _block_diag2(w):
    z = jnp.zeros_like(w[0])
    return jnp.concatenate([jnp.concatenate([w[0], z], axis=1), jnp.concatenate([z, w[1]], axis=1)], axis=0)


def _layer(h_in, mem2, batch, seq, mem_len, lng, lnb, first, p):
    del first
    n, d = h_in.shape
    (w_in, rwkv_mu, rwkv_w0, rwkv_w2, rwkv_a0, rwkv_a2, rwkv_g2, rwkv_k_k, rwkv_k_a, rwkv_r_k,
     rwkv_gn_g, rwkv_gn_b, gmlp_ln_g, gmlp_ln_b, gmlp_w_s, gmlp_b_s, w_branch, w_mix_out, ln1_g, ln1_b,
     mem_ln_g, mem_ln_b, xattn_w_q, xattn_w_kv, xattn_w_o, ln2_g, ln2_b,
     peer_w_query, peer_sub_keys, peer_u, peer_v, ln3_g, ln3_b, alpha) = p
    width = rwkv_k_k.shape[0]
    rw_cols = rwkv_mu.shape[0]
    gm_cols = 2 * gmlp_ln_g.shape[0]
    chunk = gmlp_w_s.shape[1]
    row1 = lambda t: t.reshape(1, -1)

    tm = 256
    wa = w_in[:, :rw_cols].astype(BF16)
    wb = w_in[:, rw_cols:rw_cols + gm_cols].astype(BF16)
    wg = w_in[:, rw_cols + gm_cols:].astype(BF16)
    pa, pb, pg = _inproj(h_in, row1(lng), row1(lnb), wa, wb, wg, tm)

    heads_r = width // HEAD_DIM
    bd = jnp.asarray(np.kron(np.eye(heads_r, dtype=np.float32), np.ones((HEAD_DIM, HEAD_DIM), np.float32)), BF16)
    r, v, na, lw, kd, bv, g, bonus = _rwkv_pre(
        pa, seq, row1(rwkv_mu), row1(rwkv_w0), _hilo(_block_diag2(rwkv_w2)), row1(rwkv_a0),
        _hilo(_block_diag2(rwkv_a2)), _hilo(rwkv_g2), row1(rwkv_k_k), row1(rwkv_k_a), row1(rwkv_r_k), bd, tm)
    yf, yb = _scan(r, v, na, lw, kd, bv, batch, seq)

    groups = gmlp_w_s.shape[0]
    wscat = jnp.concatenate([gmlp_w_s[0::2], gmlp_w_s[1::2]], axis=2).astype(BF16)
    bsfull = jnp.repeat(gmlp_b_s.T, HEAD_DIM, axis=1)
    assert groups * HEAD_DIM == width
    h1 = _mixer(h_in, yf, yb, g, bonus, pb, pg, row1(lng), row1(lnb), row1(rwkv_gn_g), row1(rwkv_gn_b), bd,
                row1(gmlp_ln_g), row1(gmlp_ln_b), wscat, bsfull, w_branch.astype(BF16), w_mix_out.astype(BF16),
                row1(ln1_g), row1(ln1_b), alpha, chunk, tm)

    kv = _memkv(mem2, row1(mem_ln_g), row1(mem_ln_b), xattn_w_kv.astype(BF16), mem_len)
    xheads = 4
    h2 = _xattn(h1, kv, xattn_w_q.astype(BF16), xattn_w_o.astype(BF16), row1(ln2_g), row1(ln2_b),
                alpha, xheads, seq, mem_len, tm)

    pheads = peer_w_query.shape[1] // (2 * LANES)
    cnt, e1, rank2, e2 = _route(h2, peer_w_query, peer_sub_keys, pheads, 256)
    h3 = _peer(h2, peer_u.astype(BF16), peer_v.T.astype(BF16), cnt, e1, rank2, e2, row1(ln3_g), row1(ln3_b),
               alpha, pheads, 256, 1024)
    return h3


def kernel(x, mem, ln_emb_g, ln_emb_b, w_in, rwkv_mu, rwkv_w0, rwkv_w2, rwkv_a0, rwkv_a2, rwkv_g2, rwkv_k_k, rwkv_k_a, rwkv_r_k, rwkv_gn_g, rwkv_gn_b, gmlp_ln_g, gmlp_ln_b, gmlp_w_s, gmlp_b_s, w_branch, w_mix_out, ln1_g, ln1_b, mem_ln_g, mem_ln_b, xattn_w_q, xattn_w_kv, xattn_w_o, ln2_g, ln2_b, peer_w_query, peer_sub_keys, peer_u, peer_v, ln3_g, ln3_b):
    batch, seq, d = x.shape
    mem_len = mem.shape[1]
    depth = w_in.shape[0]
    assert depth == 1, "the layer pipeline applies the embedding LN inside the first layer only"
    alpha = float((2.0 * depth) ** 0.25)
    x2 = x.reshape(batch * seq, d)
    mem2 = mem.reshape(batch * mem_len, d)
    flat = lambda t: t.reshape(-1)
    p = (w_in[0], rwkv_mu[0], flat(rwkv_w0[0]), rwkv_w2[0], flat(rwkv_a0[0]), rwkv_a2[0], rwkv_g2[0],
         rwkv_k_k[0], rwkv_k_a[0], flat(rwkv_r_k[0]), rwkv_gn_g[0], rwkv_gn_b[0], gmlp_ln_g[0], gmlp_ln_b[0],
         gmlp_w_s[0], gmlp_b_s[0], w_branch[0], w_mix_out[0], ln1_g[0], ln1_b[0], mem_ln_g[0], mem_ln_b[0],
         xattn_w_q[0], xattn_w_kv[0], xattn_w_o[0], ln2_g[0], ln2_b[0], peer_w_query[0], peer_sub_keys[0],
         peer_u[0], peer_v[0], ln3_g[0], ln3_b[0], alpha)
    h = _layer(x2, mem2, batch, seq, mem_len, ln_emb_g, ln_emb_b, True, p)
    return h.reshape(batch, seq, d)
```

```python
import functools

import numpy as np
import jax
import jax.numpy as jnp
from jax import lax
from jax.experimental import pallas as pl
from jax.experimental.pallas import tpu as pltpu

F32 = jnp.float32
BF16 = jnp.bfloat16
HIGHEST = lax.Precision.HIGHEST

LN_EPS = 1e-5
GN_EPS = 64e-5
LANES = 128
HEAD_DIM = 64
SCAN_CHUNK = 64
TOPK = 16
N_KEYS = 128
VMEM_LIMIT = 56 * 1024 * 1024


def _cparams(*sem):
    return pltpu.CompilerParams(dimension_semantics=sem, vmem_limit_bytes=VMEM_LIMIT)


def _ln(x, g, b, eps=LN_EPS):
    mu = jnp.mean(x, axis=-1, keepdims=True)
    xc = x - mu
    var = jnp.mean(xc * xc, axis=-1, keepdims=True)
    return xc * lax.rsqrt(var + eps) * g + b


def _mm(a, b):
    return jnp.dot(a.astype(BF16), b.astype(BF16), preferred_element_type=F32)


def _mm_nt(a, b):
    return lax.dot_general(a, b, (((1,), (1,)), ((), ())), preferred_element_type=F32)


def _mm_nt_hi(a, b):
    return lax.dot_general(a, b, (((1,), (1,)), ((), ())), precision=HIGHEST,
                           preferred_element_type=F32)


def _split_bf16(x):
    hi = x.astype(BF16)
    return hi, (x - hi.astype(F32)).astype(BF16)


def _hilo(w):
    hi, lo = _split_bf16(w)
    return jnp.stack([hi, lo])


def _mm_x3(x, w_ref):
    hi, lo = _split_bf16(x)
    w_hi = w_ref[0]
    return (jnp.dot(hi, w_hi, preferred_element_type=F32) + jnp.dot(hi, w_ref[1], preferred_element_type=F32)
            + jnp.dot(lo, w_hi, preferred_element_type=F32))


def _segsum(x, ones):
    hi, lo = _split_bf16(x)
    return jnp.dot(hi, ones, preferred_element_type=F32) + jnp.dot(lo, ones, preferred_element_type=F32)


def _gelu(x):
    return 0.5 * x * (1.0 + lax.erf(x * np.float32(1.0 / np.sqrt(2.0))))


def _sigmoid(x):
    return 1.0 / (1.0 + jnp.exp(-x))


def _full(shape):
    nd = len(shape)
    return pl.BlockSpec(shape, lambda *_: (0,) * nd)


def _inproj_kernel(x_ref, g_ref, b_ref, wa_ref, wb_ref, wg_ref, pa_ref, pb_ref, pg_ref):
    h = _ln(x_ref[...], g_ref[...], b_ref[...]).astype(BF16)
    pa_ref[...] = jnp.dot(h, wa_ref[...], preferred_element_type=F32)
    pb_ref[...] = jnp.dot(h, wb_ref[...], preferred_element_type=F32)
    pg_ref[...] = jnp.dot(h, wg_ref[...], preferred_element_type=F32)


def _inproj(x2, g, b, wa, wb, wg, tm):
    n, d = x2.shape
    ca, cb, cg = wa.shape[1], wb.shape[1], wg.shape[1]
    row = lambda c: pl.BlockSpec((tm, c), lambda i: (i, 0))
    return pl.pallas_call(
        _inproj_kernel,
        grid=(n // tm,),
        in_specs=[row(d), _full((1, d)), _full((1, d)), _full((d, ca)), _full((d, cb)), _full((d, cg))],
        out_specs=[row(ca), row(cb), row(cg)],
        out_shape=[jax.ShapeDtypeStruct((n, c), F32) for c in (ca, cb, cg)],
        compiler_params=_cparams("parallel"),
        name="inproj",
    )(x2, g, b, wa, wb, wg)


def _rwkv_pre_kernel(tiles_per_seq, width,
                     p_ref, prev_ref, next_ref, mu_ref, w0_ref, w2_ref, a0_ref, a2_ref, g2_ref,
                     kk_ref, ka_ref, rk_ref, bd_ref,
                     r_out, v_out, na_out, lw_out, kd_out, bv_out, g_out, bonus_out):
    i = pl.program_id(0)
    pos = i % tiles_per_seq
    p = p_ref[...]
    tm = p.shape[0]
    row = lax.broadcasted_iota(jnp.int32, (tm, 1), 0)
    prev_row = jnp.where(pos == 0, 0.0, prev_ref[7:8, :])
    next_row = jnp.where(pos == tiles_per_seq - 1, 0.0, next_ref[0:1, :])
    prev = jnp.where(row == 0, prev_row, pltpu.roll(p, 1, 0))
    nxt = jnp.where(row == tm - 1, next_row, pltpu.roll(p, tm - 1, 0))
    ps = p + mu_ref[...] * (0.5 * (prev + nxt) - p)

    w = width
    r = ps[:, 0:w]
    k = ps[:, w:2 * w]
    v = ps[:, 2 * w:3 * w]
    wd = ps[:, 3 * w:3 * w + LANES]
    ad = ps[:, 3 * w + LANES:3 * w + 2 * LANES]
    gd = ps[:, 3 * w + 2 * LANES:3 * w + 3 * LANES]

    wz = w0_ref[...] + _mm_x3(jnp.tanh(wd), w2_ref)
    w_log = -(jnp.maximum(-wz, 0.0) + jnp.log(1.0 + jnp.exp(-jnp.abs(wz)))) - 0.5
    log_decay = -jnp.exp(w_log)
    a = _sigmoid(a0_ref[...] + _mm_x3(ad, a2_ref))
    g = _mm_x3(_sigmoid(gd), g2_ref)

    bd = bd_ref[...]
    kk = k * kk_ref[...]
    nrm = jnp.sqrt(_segsum(kk * kk, bd))
    kk = kk / jnp.maximum(nrm, 1e-12)

    ka = ka_ref[...]
    k0 = k * (1.0 + (a[:, 0:w] - 1.0) * ka)
    k1 = k * (1.0 + (a[:, w:2 * w] - 1.0) * ka)
    rk = _segsum(r * (k0 + k1) * rk_ref[...], bd)

    r_out[...] = r
    v_out[...] = v
    na_out[...] = -kk
    lw_out[0] = log_decay[:, 0:w]
    lw_out[1] = log_decay[:, w:2 * w]
    kd_out[0] = k0
    kd_out[1] = k1
    bv_out[0] = kk * a[:, 0:w]
    bv_out[1] = kk * a[:, w:2 * w]
    g_out[...] = g
    bonus_out[...] = rk * v


def _rwkv_pre(pa, seq, mu, w0, w2bd, a0, a2bd, g2, k_k, k_a, r_k, bd, tm):
    n, ca = pa.shape
    w = bd.shape[0]
    tps = seq // tm
    hb = tm // 8
    nb8 = n // 8
    row = lambda c: pl.BlockSpec((tm, c), lambda i: (i, 0))
    row2 = pl.BlockSpec((2, tm, w), lambda i: (0, i, 0))
    outs = [jax.ShapeDtypeStruct((n, w), F32)] * 3 + [jax.ShapeDtypeStruct((2, n, w), F32)] * 3 + \
           [jax.ShapeDtypeStruct((n, w), F32)] * 2
    return pl.pallas_call(
        functools.partial(_rwkv_pre_kernel, tps, w),
        grid=(n // tm,),
        in_specs=[row(ca),
                  pl.BlockSpec((8, ca), lambda i: (jnp.maximum(i * hb - 1, 0), 0)),
                  pl.BlockSpec((8, ca), lambda i: (jnp.minimum((i + 1) * hb, nb8 - 1), 0)),
                  _full((1, ca)), _full((1, 2 * w)), _full(w2bd.shape), _full((1, 2 * w)),
                  _full(a2bd.shape), _full(g2.shape), _full((1, w)), _full((1, w)), _full((1, w)),
                  _full((w, w))],
        out_specs=[row(w), row(w), row(w), row2, row2, row2, row(w), row(w)],
        out_shape=outs,
        compiler_params=_cparams("parallel"),
        name="rwkv_pre",
    )(pa, pa, pa, mu, w0, w2bd, a0, a2bd, g2, k_k, k_a, r_k, bd)


def _scan_kernel(rf_ref, vf_ref, af_ref, rb_ref, vb_ref, ab_ref, lwf_ref, kf_ref, bf_ref, lwb_ref, kb_ref, bb_ref,
                 ms_ref, mi_ref, lc_ref, yf_ref, yb_ref, s_scr):
    c = SCAN_CHUNK
    h2 = 2 * c

    @pl.when(pl.program_id(1) == 0)
    def _():
        s_scr[...] = jnp.zeros_like(s_scr)

    first_head = lax.broadcasted_iota(jnp.int32, (1, LANES), 1) < HEAD_DIM

    def stack(x):
        return jnp.concatenate([jnp.where(first_head, x, 0.0), jnp.where(first_head, 0.0, x)], axis=0)

    dot = functools.partial(jnp.dot, preferred_element_type=F32)
    n_groups = lwf_ref.shape[1] // LANES
    sls = [slice(m * LANES, (m + 1) * LANES) for m in range(n_groups)]

    ar, bk, v_s, bk_e, p_end, strict, incl = [], [], [], [], [], [], []
    dirs = ((rf_ref, vf_ref, af_ref, lwf_ref, kf_ref, bf_ref), (rb_ref, vb_ref, ab_ref, lwb_ref, kb_ref, bb_ref))
    for d, (r_ref, v_ref, a_ref, lw_ref, k_ref, b_ref) in enumerate(dirs):
        lw = lw_ref[...]
        l1 = lw.astype(BF16)
        rest = lw - l1.astype(F32)
        l2 = rest.astype(BF16)
        l3 = (rest - l2.astype(F32)).astype(BF16)
        lc = lc_ref[d]
        cum = dot(lc, l1) + dot(lc, l2) + dot(lc, l3)
        tot = jnp.sum(lw, axis=0, keepdims=True)
        e_neg = jnp.exp(-cum)
        e_end = jnp.exp(tot - cum)
        a_t = a_ref[...] * jnp.exp(cum - lw)
        r_t = r_ref[...] * jnp.exp(cum)
        b_in = b_ref[...]
        k_in = k_ref[...]
        b_t = b_in * e_neg
        k_t = k_in * e_neg
        b_e = b_in * e_end
        k_e = k_in * e_end
        v_in = v_ref[...]
        pe = jnp.exp(tot)
        sd = ms_ref[d] > 0.0
        ic = mi_ref[d] > 0.0
        for s in sls:
            ar.append(jnp.concatenate([stack(a_t[:, s]), stack(r_t[:, s])], axis=0).astype(BF16))
            bk.append(jnp.concatenate([stack(b_t[:, s]), stack(k_t[:, s])], axis=0).astype(BF16))
            bk_e.append(jnp.concatenate([stack(b_e[:, s]), stack(k_e[:, s])], axis=0).astype(BF16))
            v_s.append(stack(v_in[:, s]))
            p_end.append(pe[:, s])
            strict.append(sd)
            incl.append(ic)
    gs = range(2 * n_groups)
    v_b = [x.astype(BF16) for x in v_s]
    state = [s_scr[m] for m in gs]

    l_all = [_mm_nt(ar[m], bk[m]) for m in gs]
    from_state = [_mm_nt(ar[m], state[m].astype(BF16)) for m in gs]
    l_ab = [jnp.where(strict[m], l_all[m][:h2, :h2], 0.0).astype(BF16) for m in gs]
    l_k = [jnp.concatenate([jnp.where(strict[m], l_all[m][:h2, h2:], 0.0),
                            jnp.where(incl[m], l_all[m][h2:, h2:], 0.0)], axis=0).astype(BF16) for m in gs]
    l_rb = [jnp.where(incl[m], l_all[m][h2:, :h2], 0.0).astype(BF16) for m in gs]
    from_v = [dot(l_k[m], v_b[m]) for m in gs]

    x = [from_state[m][:h2] + from_v[m][:h2] for m in gs]
    lp = l_ab
    x = [x[m] + dot(lp[m], x[m].astype(BF16)) for m in gs]
    span = 2
    while span < c:
        lp = [dot(lp[m], lp[m]).astype(BF16) for m in gs]
        x = [x[m] + dot(lp[m], x[m].astype(BF16)) for m in gs]
        span *= 2
    u_s = x

    y_s = [from_state[m][h2:] + from_v[m][h2:] + dot(l_rb[m], u_s[m].astype(BF16)) for m in gs]
    uv_t = [jnp.transpose(jnp.concatenate([u_s[m], v_s[m]], axis=0)).astype(BF16) for m in gs]
    new_state = [state[m] * p_end[m] + dot(uv_t[m], bk_e[m]) for m in gs]
    for m in gs:
        y_ref = yf_ref if m < n_groups else yb_ref
        y_ref[:, sls[m % n_groups]] = y_s[m][:c] + y_s[m][c:]
        s_scr[m] = new_state[m]


def _scan(r, v, na, lw, kd, bv, batch, seq):
    n, w = r.shape
    c = SCAN_CHUNK
    nc = seq // c
    t = np.arange(c)
    before = [(t[None, :] < t[:, None]), (t[None, :] > t[:, None])]
    eye = np.eye(c, dtype=bool)
    tile2 = lambda m: np.tile(m, (2, 2))
    ms = jnp.asarray(np.stack([tile2(m) for m in before]).astype(np.float32))
    mi = jnp.asarray(np.stack([tile2(m | eye) for m in before]).astype(np.float32))
    lc = jnp.asarray(np.stack([(m | eye) for m in before]).astype(np.float32), BF16)

    fwd = pl.BlockSpec((c, w), lambda b, ci: (b * nc + ci, 0))
    bwd = pl.BlockSpec((c, w), lambda b, ci: (b * nc + nc - 1 - ci, 0))
    fwd_d = pl.BlockSpec((None, c, w), lambda b, ci: (0, b * nc + ci, 0))
    bwd_d = pl.BlockSpec((None, c, w), lambda b, ci: (1, b * nc + nc - 1 - ci, 0))
    return pl.pallas_call(
        _scan_kernel,
        grid=(batch, nc),
        in_specs=[fwd, fwd, fwd, bwd, bwd, bwd, fwd_d, fwd_d, fwd_d, bwd_d, bwd_d, bwd_d,
                  _full(ms.shape), _full(mi.shape), _full(lc.shape)],
        out_specs=[fwd, bwd],
        out_shape=[jax.ShapeDtypeStruct((n, w), F32)] * 2,
        scratch_shapes=[pltpu.VMEM((2 * (w // LANES), LANES, LANES), F32)],
        compiler_params=_cparams("parallel", "arbitrary"),
        name="scan",
    )(r, v, na, r, v, na, lw, kd, bv, lw, kd, bv, ms, mi, lc)


def _mixer_kernel(alpha, chunk,
                  x_ref, yf_ref, yb_ref, g_ref, bonus_ref, pb_ref, pg_ref,
                  lng_ref, lnb_ref, gng_ref, gnb_ref, bd_ref, mlg_ref, mlb_ref, ws_ref, bs_ref,
                  wbr_ref, wmix_ref, l1g_ref, l1b_ref, o_ref):
    h0 = _ln(x_ref[...], lng_ref[...], lnb_ref[...])
    bd = bd_ref[...]
    inv = np.float32(1.0 / HEAD_DIM)

    y = yf_ref[...] + yb_ref[...]
    mu = _segsum(y, bd) * inv
    yc = y - mu
    var = _segsum(yc * yc, bd) * inv
    y_a = (yc * lax.rsqrt(var + GN_EPS) * gng_ref[...] + gnb_ref[...] + bonus_ref[...]) * g_ref[...]

    gp = _gelu(pb_ref[...])
    w = gp.shape[1] // 2
    u = gp[:, :w]
    vn = _ln(gp[:, w:], mlg_ref[...], mlb_ref[...])
    tm = u.shape[0]
    first_group = lax.broadcasted_iota(jnp.int32, (1, LANES), 1) < HEAD_DIM
    rows = []
    for ci in range(tm // chunk):
        vc = vn[ci * chunk:(ci + 1) * chunk]
        cols = []
        for m in range(w // LANES):
            vp = vc[:, m * LANES:(m + 1) * LANES]
            stacked = jnp.concatenate([jnp.where(first_group, vp, 0.0), jnp.where(first_group, 0.0, vp)], axis=0)
            cols.append(_mm(ws_ref[m], stacked))
        rows.append(jnp.concatenate(cols, axis=1) + bs_ref[...])
    sv = jnp.concatenate(rows, axis=0)
    y_b = u * sv

    pg = pg_ref[...]
    d = pg.shape[1] // 2
    merged = _sigmoid(pg[:, :d]) * _mm(y_a, wbr_ref[0]) + _sigmoid(pg[:, d:]) * _mm(y_b, wbr_ref[1])
    mix = _mm(merged, wmix_ref[...])
    o_ref[...] = _ln(alpha * h0 + mix, l1g_ref[...], l1b_ref[...])


def _mixer(x2, yf, yb, g, bonus, pb, pg, lng, lnb, gng, gnb, bd, mlg, mlb, wscat, bsfull, wbr, wmix, l1g, l1b,
           alpha, chunk, tm):
    n, d = x2.shape
    w = g.shape[1]
    row = lambda c: pl.BlockSpec((tm, c), lambda i: (i, 0))
    return pl.pallas_call(
        functools.partial(_mixer_kernel, alpha, chunk),
        grid=(n // tm,),
        in_specs=[row(d), row(w), row(w), row(w), row(w), row(2 * w), row(2 * d),
                  _full((1, d)), _full((1, d)), _full((1, w)), _full((1, w)), _full((w, w)),
                  _full((1, w)), _full((1, w)), _full(wscat.shape), _full(bsfull.shape),
                  _full(wbr.shape), _full(wmix.shape), _full((1, d)), _full((1, d))],
        out_specs=row(d),
        out_shape=jax.ShapeDtypeStruct((n, d), F32),
        compiler_params=_cparams("parallel"),
        name="mixer",
    )(x2, yf, yb, g, bonus, pb, pg, lng, lnb, gng, gnb, bd, mlg, mlb, wscat, bsfull, wbr, wmix, l1g, l1b)


def _memkv_kernel(m_ref, g_ref, b_ref, w_ref, o_ref):
    o_ref[...] = _mm(_ln(m_ref[...], g_ref[...], b_ref[...]), w_ref[...]).astype(BF16)


def _memkv(mem2, g, b, wkv, tm):
    n, d = mem2.shape
    return pl.pallas_call(
        _memkv_kernel,
        grid=(n // tm,),
        in_specs=[pl.BlockSpec((tm, d), lambda i: (i, 0)), _full((1, d)), _full((1, d)), _full(wkv.shape)],
        out_specs=pl.BlockSpec((tm, wkv.shape[1]), lambda i: (i, 0)),
        out_shape=jax.ShapeDtypeStruct((n, wkv.shape[1]), BF16),
        compiler_params=_cparams("parallel"),
        name="memkv",
    )(mem2, g, b, wkv)


def _xattn_kernel(alpha, heads, h_ref, kv_ref, wq_ref, wo_ref, g_ref, b_ref, o_ref):
    h = h_ref[...]
    d = h.shape[1]
    hd = d // heads
    q = _mm(h, wq_ref[...])
    kv = kv_ref[...]
    scale = np.float32(hd ** -0.5)
    hs = range(heads)
    qb = q.astype(BF16)
    s = [_mm_nt(qb[:, i * hd:(i + 1) * hd], kv[:, i * hd:(i + 1) * hd]) * scale for i in hs]
    s = [x - jnp.max(x, axis=-1, keepdims=True) for x in s]
    e = [jnp.exp(x) for x in s]
    prob = [x / jnp.sum(x, axis=-1, keepdims=True) for x in e]
    o = [_mm(prob[i], kv[:, d + i * hd:d + (i + 1) * hd]) for i in hs]
    out = _mm(jnp.concatenate(o, axis=1), wo_ref[...])
    o_ref[...] = _ln(alpha * h + out, g_ref[...], b_ref[...])


def _xattn(h1, kv, wq, wo, g, b, alpha, heads, seq, mem_len, tm):
    n, d = h1.shape
    tps = seq // tm
    return pl.pallas_call(
        functools.partial(_xattn_kernel, alpha, heads),
        grid=(n // tm,),
        in_specs=[pl.BlockSpec((tm, d), lambda i: (i, 0)),
                  pl.BlockSpec((mem_len, 2 * d), lambda i: (i // tps, 0)),
                  _full(wq.shape), _full(wo.shape), _full((1, d)), _full((1, d))],
        out_specs=pl.BlockSpec((tm, d), lambda i: (i, 0)),
        out_shape=jax.ShapeDtypeStruct((n, d), F32),
        compiler_params=_cparams("parallel"),
        name="xattn",
    )(h1, kv, wq, wo, g, b)


def _extract_topk(s, pos, k):
    rank = jnp.full(s.shape, float(k), F32)
    vals = []
    for j in range(k):
        m = jnp.max(s, axis=0, keepdims=True)
        first = jnp.min(jnp.where(s == m, pos, np.float32(np.inf)), axis=0, keepdims=True)
        sel = pos == first
        rank = jnp.where(sel, float(j), rank)
        s = jnp.where(sel, -jnp.inf, s)
        vals.append(m)
    return vals, rank


def _bf16_pair_word(x):
    hi = pltpu.bitcast(x.astype(BF16).astype(F32), jnp.int32) & jnp.int32(-65536)
    return hi | lax.shift_right_logical(hi, 16)


def _candidate_layout(k):
    pieces = [("row", 0, 0, 16), ("row", 1, 0, 8), ("col", 0, 0, 16), ("col", 1, 0, 8),
              ("col", 2, 0, 8), ("col", 3, 0, 8), ("col", 4, 0, 8)]
    pos, seen = [], set()
    for kind, fixed, start, n in pieces:
        for o in range(start, start + n):
            i, j = (fixed, o) if kind == "row" else (o, fixed)
            ok = (i + 1) * (j + 1) <= k and (i, j) not in seen
            if ok:
                seen.add((i, j))
            pos.append(float(i * k + j) if ok else -1.0)
    assert len(seen) == sum(k // (i + 1) for i in range(k))
    return pieces, np.asarray(pos, np.float32)


def _sort16_comparators():
    def merge(lo, hi, r):
        step = r * 2
        if step < hi - lo:
            yield from merge(lo, hi, step)
            yield from merge(lo + r, hi, step)
            yield from ((i, i + r) for i in range(lo + r, hi - r, step))
        else:
            yield (lo, lo + r)

    def sort(lo, hi):
        if hi - lo >= 1:
            mid = lo + (hi - lo) // 2
            yield from sort(lo, mid)
            yield from sort(mid + 1, hi)
            yield from merge(lo, hi, 1)

    return tuple(sort(0, 15))


def _top16_sorted(s):
    def exchange(v, i, j):
        v[i], v[j] = jnp.maximum(v[i], v[j]), jnp.minimum(v[i], v[j])

    v = [s[8 * r:8 * r + 8] for r in range(16)]
    for i, j in _sort16_comparators():
        exchange(v, i, j)
    for shift in (4, 2, 1):
        w = [pltpu.roll(x, shift, 0) for x in v]
        v = [jnp.maximum(v[i], w[15 - i]) for i in range(16)]
        for stride in (8, 4, 2, 1):
            for i in range(16):
                if not i & stride:
                    exchange(v, i, i + stride)
    return v


def _route_kernel(heads, h_ref, wq_ref, sk_ref, cpos_ref, cnt_ref, e1_ref, rank2_ref, e2_ref, q_scr):
    h = h_ref[...]
    tt = h.shape[0]
    q_scr[...] = _mm(h, wq_ref[...])
    k = TOPK
    pieces, _ = _candidate_layout(k)
    cpos = cpos_ref[...]
    cvalid = cpos >= 0.0
    cpos_inf = jnp.where(cvalid, cpos, np.float32(np.inf))
    key_pos = lax.broadcasted_iota(jnp.int32, (N_KEYS, tt), 0).astype(F32)
    row16 = lax.broadcasted_iota(jnp.int32, (k, 1), 0)

    def scores(hd):
        off = pl.multiple_of(hd * (2 * LANES), 2 * LANES)
        s1 = _mm_nt_hi(sk_ref[0], q_scr[:, pl.ds(off, LANES)])
        s2 = _mm_nt_hi(sk_ref[1], q_scr[:, pl.ds(off + LANES, LANES)])
        return s1, s2

    def pair_selection(top1, top2):
        t1 = jnp.concatenate(top1, axis=0)
        t2 = jnp.concatenate(top2, axis=0)
        parts = []
        for kind, fixed, start, n in pieces:
            if kind == "row":
                parts.append(top1[fixed] + t2[start:start + n])
            else:
                parts.append(t1[start:start + n] + top2[fixed])
        cand = jnp.where(cvalid, jnp.concatenate(parts, axis=0), -jnp.inf)
        best, crank = _extract_topk(cand, cpos_inf, k)
        chosen = jnp.where(crank < float(k), 1.0, 0.0)
        n16 = jnp.zeros((k, tt), F32)
        r0 = 0
        for kind, fixed, start, n in pieces:
            c = chosen[r0:r0 + n]
            r0 += n
            if kind == "row":
                n16 = n16 + jnp.where(row16 == fixed, jnp.sum(c, axis=0, keepdims=True), 0.0)
            elif n == k:
                n16 = n16 + c
            else:
                n16 = n16 + jnp.concatenate([c, jnp.zeros((k - n, tt), F32)], axis=0)
        return best, n16

    def emit(hd, s1, s2, max1, max2, best, cnt_dense, rank2):
        z = jnp.zeros_like(best[0])
        for j in range(k):
            z = z + jnp.exp(best[j] - best[0])
        cnt_ref[hd] = _bf16_pair_word(cnt_dense)
        e1_ref[hd] = jnp.exp(s1 - max1) / z
        rank2_ref[hd] = rank2.astype(BF16)
        e2_ref[hd] = jnp.exp(s2 - max2).astype(BF16)

    def head_by_sorting(hd, bad):
        s1, s2 = scores(hd)
        v1 = _top16_sorted(s1)
        v2 = _top16_sorted(s2)
        top1 = [x[0:1] for x in v1]
        top2 = [x[0:1] for x in v2]
        best, n16 = pair_selection(top1, top2)
        rank2 = jnp.zeros_like(s2)
        cnt_dense = jnp.zeros_like(s1)
        for i in range(k):
            rank2 = rank2 + jnp.where(top2[i] > s2, 1.0, 0.0)
            cnt_dense = cnt_dense + jnp.where(s1 == top1[i], n16[i:i + 1], 0.0)
        in1 = jnp.sum(jnp.where(s1 >= top1[k - 1], 1.0, 0.0), axis=0, keepdims=True)
        in2 = jnp.sum(jnp.where(rank2 < float(k), 1.0, 0.0), axis=0, keepdims=True)
        ties = jnp.where(in1 == float(k), 0.0, 1.0) + jnp.where(in2 == float(k), 0.0, 1.0)
        for i in range(k - 1):
            ties = ties + jnp.where(top1[i] > top1[i + 1], 0.0, 1.0) + jnp.where(top2[i] > top2[i + 1], 0.0, 1.0)
        emit(hd, s1, s2, top1[0], top2[0], best, cnt_dense, rank2)
        return bad + ties

    def head_exact(hd, carry):
        s1, s2 = scores(hd)
        top1, rank1 = _extract_topk(s1, key_pos, k)
        top2, rank2 = _extract_topk(s2, key_pos, k)
        best, n16 = pair_selection(top1, top2)
        cnt_dense = jnp.zeros_like(s1)
        for i in range(k):
            cnt_dense = cnt_dense + jnp.where(rank1 == float(i), n16[i:i + 1], 0.0)
        emit(hd, s1, s2, top1[0], top2[0], best, cnt_dense, rank2)
        return carry

    bad = lax.fori_loop(0, heads, head_by_sorting, jnp.zeros((1, tt), F32))

    @pl.when(jnp.sum(bad) > 0.0)
    def _():
        lax.fori_loop(0, heads, head_exact, 0)


def _route(h2, wq, sk, heads, tt):
    n, d = h2.shape
    _, pos = _candidate_layout(TOPK)
    cpos = jnp.asarray(np.broadcast_to(pos[:, None], (pos.shape[0], tt)).copy())
    blk = pl.BlockSpec((heads, N_KEYS, tt), lambda i: (0, 0, i))
    return pl.pallas_call(
        functools.partial(_route_kernel, heads),
        grid=(n // tt,),
        in_specs=[pl.BlockSpec((tt, d), lambda i: (i, 0)), _full(wq.shape), _full(sk.shape), _full(cpos.shape)],
        out_specs=[blk] * 4,
        out_shape=[jax.ShapeDtypeStruct((heads, N_KEYS, n), jnp.int32), jax.ShapeDtypeStruct((heads, N_KEYS, n), F32)]
                  + [jax.ShapeDtypeStruct((heads, N_KEYS, n), BF16)] * 2,
        scratch_shapes=[pltpu.VMEM((tt, wq.shape[1]), F32)],
        compiler_params=_cparams("parallel"),
        name="route",
    )(h2, wq.astype(BF16), sk, cpos)


def _peer_kernel(alpha, heads, n_pairs, h_ref, ua_ref, ub_ref, vtp_ref, vtc_ref,
                 cnt_ref, e1_ref, rank2_ref, e2_ref, g_ref, b_ref, o_ref, acc_scr, wa_scr, wb_scr, hb_scr):
    p = pl.program_id(1)
    te = ua_ref.shape[0]
    tt = h_ref.shape[0]
    blocks = te // N_KEYS
    sub = 16
    grp = N_KEYS // sub
    dot = functools.partial(jnp.dot, preferred_element_type=F32)

    @pl.when(p == 0)
    def _():
        acc_scr[...] = jnp.zeros_like(acc_scr)
        wb_scr[...] = jnp.zeros_like(wb_scr)
        hb_scr[...] = h_ref[...].astype(BF16)

    def gate_tile(u_ref, tile, w_out):
        act_all = lax.dot_general(u_ref[...], hb_scr[...], (((1,), (1,)), ((), ())),
                                  preferred_element_type=F32)
        for il in range(blocks):
            ig = tile * blocks + il
            act = act_all[il * N_KEYS:(il + 1) * N_KEYS]
            gate = jnp.zeros((grp, sub, tt), BF16)
            for hd in range(heads):
                cnt = pltpu.bitcast(jnp.broadcast_to(cnt_ref[hd, pl.ds(ig, 1), :], (sub // 2, tt)), BF16)
                e1 = jnp.broadcast_to(e1_ref[hd, pl.ds(ig, 1), :], (sub, tt)).astype(BF16)
                rank2 = rank2_ref[hd].reshape(grp, sub, tt)
                e2 = e2_ref[hd].reshape(grp, sub, tt)
                gate = gate + jnp.where(rank2 < cnt[None], e2, jnp.zeros_like(e2)) * e1[None]
            w = gate * _gelu(act).astype(BF16).reshape(grp, sub, tt)
            w_out[il * N_KEYS:(il + 1) * N_KEYS, :] = w.reshape(N_KEYS, tt)

    @pl.when(p < n_pairs)
    def _():
        acc_scr[...] += dot(vtp_ref[...], wb_scr[...])
        gate_tile(ua_ref, 2 * p, wa_scr)
        acc_scr[...] += dot(vtc_ref[...], wa_scr[...])
        gate_tile(ub_ref, 2 * p + 1, wb_scr)

    @pl.when(p == n_pairs)
    def _():
        y = jnp.transpose(acc_scr[...] + dot(vtp_ref[...], wb_scr[...]))
        o_ref[...] = _ln(alpha * h_ref[...] + y, g_ref[...], b_ref[...])


def _peer(h2, u_bf, vt_bf, cnt, e1, rank2, e2, g, b, alpha, heads, tt, te):
    n, d = h2.shape
    ne = u_bf.shape[0] // te
    n_pairs = ne // 2
    assert ne == 2 * n_pairs and ne >= 2
    blk = pl.BlockSpec((heads, N_KEYS, tt), lambda i, p: (0, 0, i))
    return pl.pallas_call(
        functools.partial(_peer_kernel, alpha, heads, n_pairs),
        grid=(n // tt, n_pairs + 1),
        in_specs=[pl.BlockSpec((tt, d), lambda i, p: (i, 0)),
                  pl.BlockSpec((te, d), lambda i, p: (jnp.minimum(2 * p, ne - 2), 0)),
                  pl.BlockSpec((te, d), lambda i, p: (jnp.minimum(2 * p + 1, ne - 1), 0)),
                  pl.BlockSpec((d, te), lambda i, p: (0, jnp.maximum(2 * p - 1, 1))),
                  pl.BlockSpec((d, te), lambda i, p: (0, jnp.minimum(2 * p, ne - 2))),
                  blk, blk, blk, blk, _full((1, d)), _full((1, d))],
        out_specs=pl.BlockSpec((tt, d), lambda i, p: (i, 0)),
        out_shape=jax.ShapeDtypeStruct((n, d), F32),
        scratch_shapes=[pltpu.VMEM((d, tt), F32), pltpu.VMEM((te, tt), BF16), pltpu.VMEM((te, tt), BF16),
                        pltpu.VMEM((tt, d), BF16)],
        compiler_params=_cparams("parallel", "arbitrary"),
        name="peer",
    )(h2, u_bf, u_bf, vt_bf, vt_bf, cnt, e1, rank2, e2, g, b)


def _block_diag2(w):
    z = jnp.zeros_like(w[0])
    return jnp.concatenate([jnp.concatenate([w[0], z], axis=1), jnp.concatenate([z, w[1]], axis=1)], axis=0)


def _layer(h_in, mem2, batch, seq, mem_len, lng, lnb, first, p):
    del first
    n, d = h_in.shape
    (w_in, rwkv_mu, rwkv_w0, rwkv_w2, rwkv_a0, rwkv_a2, rwkv_g2, rwkv_k_k, rwkv_k_a, rwkv_r_k,
     rwkv_gn_g, rwkv_gn_b, gmlp_ln_g, gmlp_ln_b, gmlp_w_s, gmlp_b_s, w_branch, w_mix_out, ln1_g, ln1_b,
     mem_ln_g, mem_ln_b, xattn_w_q, xattn_w_kv, xattn_w_o, ln2_g, ln2_b,
     peer_w_query, peer_sub_keys, peer_u, peer_v, ln3_g, ln3_b, alpha) = p
    width = rwkv_k_k.shape[0]
    rw_cols = rwkv_mu.shape[0]
    gm_cols = 2 * gmlp_ln_g.shape[0]
    chunk = gmlp_w_s.shape[1]
    row1 = lambda t: t.reshape(1, -1)

    tm = 256
    wa = w_in[:, :rw_cols].astype(BF16)
    wb = w_in[:, rw_cols:rw_cols + gm_cols].astype(BF16)
    wg = w_in[:, rw_cols + gm_cols:].astype(BF16)
    pa, pb, pg = _inproj(h_in, row1(lng), row1(lnb), wa, wb, wg, tm)

    heads_r = width // HEAD_DIM
    bd = jnp.asarray(np.kron(np.eye(heads_r, dtype=np.float32), np.ones((HEAD_DIM, HEAD_DIM), np.float32)), BF16)
    r, v, na, lw, kd, bv, g, bonus = _rwkv_pre(
        pa, seq, row1(rwkv_mu), row1(rwkv_w0), _hilo(_block_diag2(rwkv_w2)), row1(rwkv_a0),
        _hilo(_block_diag2(rwkv_a2)), _hilo(rwkv_g2), row1(rwkv_k_k), row1(rwkv_k_a), row1(rwkv_r_k), bd, tm)
    yf, yb = _scan(r, v, na, lw, kd, bv, batch, seq)

    groups = gmlp_w_s.shape[0]
    wscat = jnp.concatenate([gmlp_w_s[0::2], gmlp_w_s[1::2]], axis=2).astype(BF16)
    bsfull = jnp.repeat(gmlp_b_s.T, HEAD_DIM, axis=1)
    assert groups * HEAD_DIM == width
    h1 = _mixer(h_in, yf, yb, g, bonus, pb, pg, row1(lng), row1(lnb), row1(rwkv_gn_g), row1(rwkv_gn_b), bd,
                row1(gmlp_ln_g), row1(gmlp_ln_b), wscat, bsfull, w_branch.astype(BF16), w_mix_out.astype(BF16),
                row1(ln1_g), row1(ln1_b), alpha, chunk, tm)

    kv = _memkv(mem2, row1(mem_ln_g), row1(mem_ln_b), xattn_w_kv.astype(BF16), mem_len)
    xheads = 4
    h2 = _xattn(h1, kv, xattn_w_q.astype(BF16), xattn_w_o.astype(BF16), row1(ln2_g), row1(ln2_b),
                alpha, xheads, seq, mem_len, tm)

    pheads = peer_w_query.shape[1] // (2 * LANES)
    cnt, e1, rank2, e2 = _route(h2, peer_w_query, peer_sub_keys, pheads, 256)
    h3 = _peer(h2, peer_u.astype(BF16), peer_v.T.astype(BF16), cnt, e1, rank2, e2, row1(ln3_g), row1(ln3_b),
               alpha, pheads, 256, 1024)
    return h3


def kernel(x, mem, ln_emb_g, ln_emb_b, w_in, rwkv_mu, rwkv_w0, rwkv_w2, rwkv_a0, rwkv_a2, rwkv_g2, rwkv_k_k, rwkv_k_a, rwkv_r_k, rwkv_gn_g, rwkv_gn_b, gmlp_ln_g, gmlp_ln_b, gmlp_w_s, gmlp_b_s, w_branch, w_mix_out, ln1_g, ln1_b, mem_ln_g, mem_ln_b, xattn_w_q, xattn_w_kv, xattn_w_o, ln2_g, ln2_b, peer_w_query, peer_sub_keys, peer_u, peer_v, ln3_g, ln3_b):
    batch, seq, d = x.shape
    mem_len = mem.shape[1]
    depth = w_in.shape[0]
    assert depth == 1, "the layer pipeline applies the embedding LN inside the first layer only"
    alpha = float((2.0 * depth) ** 0.25)
    x2 = x.reshape(batch * seq, d)
    mem2 = mem.reshape(batch * mem_len, d)
    flat = lambda t: t.reshape(-1)
    p = (w_in[0], rwkv_mu[0], flat(rwkv_w0[0]), rwkv_w2[0], flat(rwkv_a0[0]), rwkv_a2[0], rwkv_g2[0],
         rwkv_k_k[0], rwkv_k_a[0], flat(rwkv_r_k[0]), rwkv_gn_g[0], rwkv_gn_b[0], gmlp_ln_g[0], gmlp_ln_b[0],
         gmlp_w_s[0], gmlp_b_s[0], w_branch[0], w_mix_out[0], ln1_g[0], ln1_b[0], mem_ln_g[0], mem_ln_b[0],
         xattn_w_q[0], xattn_w_kv[0], xattn_w_o[0], ln2_g[0], ln2_b[0], peer_w_query[0], peer_sub_keys[0],
         peer_u[0], peer_v[0], ln3_g[0], ln3_b[0], alpha)
    h = _layer(x2, mem2, batch, seq, mem_len, ln_emb_g, ln_emb_b, True, p)
    return h.reshape(batch, seq, d)
```

```python
import functools

import numpy as np
import jax
import jax.numpy as jnp
from jax import lax
from jax.experimental import pallas as pl
from jax.experimental.pallas import tpu as pltpu

F32 = jnp.float32
BF16 = jnp.bfloat16
HIGHEST = lax.Precision.HIGHEST

LN_EPS = 1e-5
GN_EPS = 64e-5
LANES = 128
HEAD_DIM = 64
SCAN_CHUNK = 64
TOPK = 16
N_KEYS = 128
VMEM_LIMIT = 56 * 1024 * 1024


def _cparams(*sem):
    return pltpu.CompilerParams(dimension_semantics=sem, vmem_limit_bytes=VMEM_LIMIT)


def _ln(x, g, b, eps=LN_EPS):
    mu = jnp.mean(x, axis=-1, keepdims=True)
    xc = x - mu
    var = jnp.mean(xc * xc, axis=-1, keepdims=True)
    return xc * lax.rsqrt(var + eps) * g + b


def _mm(a, b):
    return jnp.dot(a.astype(BF16), b.astype(BF16), preferred_element_type=F32)


def _mm_nt(a, b):
    return lax.dot_general(a, b, (((1,), (1,)), ((), ())), preferred_element_type=F32)


def _mm_nt_hi(a, b):
    return lax.dot_general(a, b, (((1,), (1,)), ((), ())), precision=HIGHEST,
                           preferred_element_type=F32)


def _split_bf16(x):
    hi = x.astype(BF16)
    return hi, (x - hi.astype(F32)).astype(BF16)


def _hilo(w):
    hi, lo = _split_bf16(w)
    return jnp.stack([hi, lo])


def _mm_x3(x, w_ref):
    hi, lo = _split_bf16(x)
    w_hi = w_ref[0]
    return (jnp.dot(hi, w_hi, preferred_element_type=F32) + jnp.dot(hi, w_ref[1], preferred_element_type=F32)
            + jnp.dot(lo, w_hi, preferred_element_type=F32))


def _segsum(x, ones):
    hi, lo = _split_bf16(x)
    return jnp.dot(hi, ones, preferred_element_type=F32) + jnp.dot(lo, ones, preferred_element_type=F32)


def _gelu(x):
    return 0.5 * x * (1.0 + lax.erf(x * np.float32(1.0 / np.sqrt(2.0))))


def _sigmoid(x):
    return 1.0 / (1.0 + jnp.exp(-x))


def _full(shape):
    nd = len(shape)
    return pl.BlockSpec(shape, lambda *_: (0,) * nd)


def _inproj_kernel(x_ref, g_ref, b_ref, wa_ref, wb_ref, wg_ref, pa_ref, pb_ref, pg_ref):
    h = _ln(x_ref[...], g_ref[...], b_ref[...]).astype(BF16)
    pa_ref[...] = jnp.dot(h, wa_ref[...], preferred_element_type=F32)
    pb_ref[...] = jnp.dot(h, wb_ref[...], preferred_element_type=F32)
    pg_ref[...] = jnp.dot(h, wg_ref[...], preferred_element_type=F32)


def _inproj(x2, g, b, wa, wb, wg, tm):
    n, d = x2.shape
    ca, cb, cg = wa.shape[1], wb.shape[1], wg.shape[1]
    row = lambda c: pl.BlockSpec((tm, c), lambda i: (i, 0))
    return pl.pallas_call(
        _inproj_kernel,
        grid=(n // tm,),
        in_specs=[row(d), _full((1, d)), _full((1, d)), _full((d, ca)), _full((d, cb)), _full((d, cg))],
        out_specs=[row(ca), row(cb), row(cg)],
        out_shape=[jax.ShapeDtypeStruct((n, c), F32) for c in (ca, cb, cg)],
        compiler_params=_cparams("parallel"),
        name="inproj",
    )(x2, g, b, wa, wb, wg)


def _rwkv_pre_kernel(tiles_per_seq, width,
                     p_ref, prev_ref, next_ref, mu_ref, w0_ref, w2_ref, a0_ref, a2_ref, g2_ref,
                     kk_ref, ka_ref, rk_ref, bd_ref,
                     r_out, v_out, na_out, lw_out, kd_out, bv_out, g_out, bonus_out):
    i = pl.program_id(0)
    pos = i % tiles_per_seq
    p = p_ref[...]
    tm = p.shape[0]
    row = lax.broadcasted_iota(jnp.int32, (tm, 1), 0)
    prev_row = jnp.where(pos == 0, 0.0, prev_ref[7:8, :])
    next_row = jnp.where(pos == tiles_per_seq - 1, 0.0, next_ref[0:1, :])
    prev = jnp.where(row == 0, prev_row, pltpu.roll(p, 1, 0))
    nxt = jnp.where(row == tm - 1, next_row, pltpu.roll(p, tm - 1, 0))
    ps = p + mu_ref[...] * (0.5 * (prev + nxt) - p)

    w = width
    r = ps[:, 0:w]
    k = ps[:, w:2 * w]
    v = ps[:, 2 * w:3 * w]
    wd = ps[:, 3 * w:3 * w + LANES]
    ad = ps[:, 3 * w + LANES:3 * w + 2 * LANES]
    gd = ps[:, 3 * w + 2 * LANES:3 * w + 3 * LANES]

    wz = w0_ref[...] + _mm_x3(jnp.tanh(wd), w2_ref)
    w_log = -(jnp.maximum(-wz, 0.0) + jnp.log(1.0 + jnp.exp(-jnp.abs(wz)))) - 0.5
    log_decay = -jnp.exp(w_log)
    a = _sigmoid(a0_ref[...] + _mm_x3(ad, a2_ref))
    g = _mm_x3(_sigmoid(gd), g2_ref)

    bd = bd_ref[...]
    kk = k * kk_ref[...]
    nrm = jnp.sqrt(_segsum(kk * kk, bd))
    kk = kk / jnp.maximum(nrm, 1e-12)

    ka = ka_ref[...]
    k0 = k * (1.0 + (a[:, 0:w] - 1.0) * ka)
    k1 = k * (1.0 + (a[:, w:2 * w] - 1.0) * ka)
    rk = _segsum(r * (k0 + k1) * rk_ref[...], bd)

    r_out[...] = r
    v_out[...] = v
    na_out[...] = -kk
    lw_out[0] = log_decay[:, 0:w]
    lw_out[1] = log_decay[:, w:2 * w]
    kd_out[0] = k0
    kd_out[1] = k1
    bv_out[0] = kk * a[:, 0:w]
    bv_out[1] = kk * a[:, w:2 * w]
    g_out[...] = g
    bonus_out[...] = rk * v


def _rwkv_pre(pa, seq, mu, w0, w2bd, a0, a2bd, g2, k_k, k_a, r_k, bd, tm):
    n, ca = pa.shape
    w = bd.shape[0]
    tps = seq // tm
    hb = tm // 8
    nb8 = n // 8
    row = lambda c: pl.BlockSpec((tm, c), lambda i: (i, 0))
    row2 = pl.BlockSpec((2, tm, w), lambda i: (0, i, 0))
    outs = [jax.ShapeDtypeStruct((n, w), F32)] * 3 + [jax.ShapeDtypeStruct((2, n, w), F32)] * 3 + \
           [jax.ShapeDtypeStruct((n, w), F32)] * 2
    return pl.pallas_call(
        functools.partial(_rwkv_pre_kernel, tps, w),
        grid=(n // tm,),
        in_specs=[row(ca),
                  pl.BlockSpec((8, ca), lambda i: (jnp.maximum(i * hb - 1, 0), 0)),
                  pl.BlockSpec((8, ca), lambda i: (jnp.minimum((i + 1) * hb, nb8 - 1), 0)),
                  _full((1, ca)), _full((1, 2 * w)), _full(w2bd.shape), _full((1, 2 * w)),
                  _full(a2bd.shape), _full(g2.shape), _full((1, w)), _full((1, w)), _full((1, w)),
                  _full((w, w))],
        out_specs=[row(w), row(w), row(w), row2, row2, row2, row(w), row(w)],
        out_shape=outs,
        compiler_params=_cparams("parallel"),
        name="rwkv_pre",
    )(pa, pa, pa, mu, w0, w2bd, a0, a2bd, g2, k_k, k_a, r_k, bd)


def _scan_kernel(rf_ref, vf_ref, af_ref, rb_ref, vb_ref, ab_ref, lwf_ref, kf_ref, bf_ref, lwb_ref, kb_ref, bb_ref,
                 ms_ref, mi_ref, lc_ref, yf_ref, yb_ref, s_scr):
    c = SCAN_CHUNK
    h2 = 2 * c

    @pl.when(pl.program_id(1) == 0)
    def _():
        s_scr[...] = jnp.zeros_like(s_scr)

    first_head = lax.broadcasted_iota(jnp.int32, (1, LANES), 1) < HEAD_DIM

    def stack(x):
        return jnp.concatenate([jnp.where(first_head, x, 0.0), jnp.where(first_head, 0.0, x)], axis=0)

    dot = functools.partial(jnp.dot, preferred_element_type=F32)
    n_groups = lwf_ref.shape[1] // LANES
    sls = [slice(m * LANES, (m + 1) * LANES) for m in range(n_groups)]

    ar, bk, v_s, bk_e, p_end, strict, incl = [], [], [], [], [], [], []
    dirs = ((rf_ref, vf_ref, af_ref, lwf_ref, kf_ref, bf_ref), (rb_ref, vb_ref, ab_ref, lwb_ref, kb_ref, bb_ref))
    for d, (r_ref, v_ref, a_ref, lw_ref, k_ref, b_ref) in enumerate(dirs):
        lw = lw_ref[...]
        l1 = lw.astype(BF16)
        rest = lw - l1.astype(F32)
        l2 = rest.astype(BF16)
        l3 = (rest - l2.astype(F32)).astype(BF16)
        lc = lc_ref[d]
        cum = dot(lc, l1) + dot(lc, l2) + dot(lc, l3)
        tot = jnp.sum(lw, axis=0, keepdims=True)
        e_neg = jnp.exp(-cum)
        e_end = jnp.exp(tot - cum)
        a_t = a_ref[...] * jnp.exp(cum - lw)
        r_t = r_ref[...] * jnp.exp(cum)
        b_in = b_ref[...]
        k_in = k_ref[...]
        b_t = b_in * e_neg
        k_t = k_in * e_neg
        b_e = b_in * e_end
        k_e = k_in * e_end
        v_in = v_ref[...]
        pe = jnp.exp(tot)
        sd = ms_ref[d] > 0.0
        ic = mi_ref[d] > 0.0
        for s in sls:
            ar.append(jnp.concatenate([stack(a_t[:, s]), stack(r_t[:, s])], axis=0).astype(BF16))
            bk.append(jnp.concatenate([stack(b_t[:, s]), stack(k_t[:, s])], axis=0).astype(BF16))
            bk_e.append(jnp.concatenate([stack(b_e[:, s]), stack(k_e[:, s])], axis=0).astype(BF16))
            v_s.append(stack(v_in[:, s]))
            p_end.append(pe[:, s])
            strict.append(sd)
            incl.append(ic)
    gs = range(2 * n_groups)
    v_b = [x.astype(BF16) for x in v_s]
    state = [s_scr[m] for m in gs]

    l_all = [_mm_nt(ar[m], bk[m]) for m in gs]
    from_state = [_mm_nt(ar[m], state[m].astype(BF16)) for m in gs]
    l_ab = [jnp.where(strict[m], l_all[m][:h2, :h2], 0.0).astype(BF16) for m in gs]
    l_k = [jnp.concatenate([jnp.where(strict[m], l_all[m][:h2, h2:], 0.0),
                            jnp.where(incl[m], l_all[m][h2:, h2:], 0.0)], axis=0).astype(BF16) for m in gs]
    l_rb = [jnp.where(incl[m], l_all[m][h2:, :h2], 0.0).astype(BF16) for m in gs]
    from_v = [dot(l_k[m], v_b[m]) for m in gs]

    x = [from_state[m][:h2] + from_v[m][:h2] for m in gs]
    lp = l_ab
    x = [x[m] + dot(lp[m], x[m].astype(BF16)) for m in gs]
    span = 2
    while span < c:
        lp = [dot(lp[m], lp[m]).astype(BF16) for m in gs]
        x = [x[m] + dot(lp[m], x[m].astype(BF16)) for m in gs]
        span *= 2
    u_s = x

    y_s = [from_state[m][h2:] + from_v[m][h2:] + dot(l_rb[m], u_s[m].astype(BF16)) for m in gs]
    uv_t = [jnp.transpose(jnp.concatenate([u_s[m], v_s[m]], axis=0)).astype(BF16) for m in gs]
    new_state = [state[m] * p_end[m] + dot(uv_t[m], bk_e[m]) for m in gs]
    for m in gs:
        y_ref = yf_ref if m < n_groups else yb_ref
        y_ref[:, sls[m % n_groups]] = y_s[m][:c] + y_s[m][c:]
        s_scr[m] = new_state[m]


def _scan(r, v, na, lw, kd, bv, batch, seq):
    n, w = r.shape
    c = SCAN_CHUNK
    nc = seq // c
    t = np.arange(c)
    before = [(t[None, :] < t[:, None]), (t[None, :] > t[:, None])]
    eye = np.eye(c, dtype=bool)
    tile2 = lambda m: np.tile(m, (2, 2))
    ms = jnp.asarray(np.stack([tile2(m) for m in before]).astype(np.float32))
    mi = jnp.asarray(np.stack([tile2(m | eye) for m in before]).astype(np.float32))
    lc = jnp.asarray(np.stack([(m | eye) for m in before]).astype(np.float32), BF16)

    fwd = pl.BlockSpec((c, w), lambda b, ci: (b * nc + ci, 0))
    bwd = pl.BlockSpec((c, w), lambda b, ci: (b * nc + nc - 1 - ci, 0))
    fwd_d = pl.BlockSpec((None, c, w), lambda b, ci: (0, b * nc + ci, 0))
    bwd_d = pl.BlockSpec((None, c, w), lambda b, ci: (1, b * nc + nc - 1 - ci, 0))
    return pl.pallas_call(
        _scan_kernel,
        grid=(batch, nc),
        in_specs=[fwd, fwd, fwd, bwd, bwd, bwd, fwd_d, fwd_d, fwd_d, bwd_d, bwd_d, bwd_d,
                  _full(ms.shape), _full(mi.shape), _full(lc.shape)],
        out_specs=[fwd, bwd],
        out_shape=[jax.ShapeDtypeStruct((n, w), F32)] * 2,
        scratch_shapes=[pltpu.VMEM((2 * (w // LANES), LANES, LANES), F32)],
        compiler_params=_cparams("parallel", "arbitrary"),
        name="scan",
    )(r, v, na, r, v, na, lw, kd, bv, lw, kd, bv, ms, mi, lc)


def _mixer_kernel(alpha, chunk,
                  x_ref, yf_ref, yb_ref, g_ref, bonus_ref, pb_ref, pg_ref,
                  lng_ref, lnb_ref, gng_ref, gnb_ref, bd_ref, mlg_ref, mlb_ref, ws_ref, bs_ref,
                  wbr_ref, wmix_ref, l1g_ref, l1b_ref, o_ref):
    h0 = _ln(x_ref[...], lng_ref[...], lnb_ref[...])
    bd = bd_ref[...]
    inv = np.float32(1.0 / HEAD_DIM)

    y = yf_ref[...] + yb_ref[...]
    mu = _segsum(y, bd) * inv
    yc = y - mu
    var = _segsum(yc * yc, bd) * inv
    y_a = (yc * lax.rsqrt(var + GN_EPS) * gng_ref[...] + gnb_ref[...] + bonus_ref[...]) * g_ref[...]

    gp = _gelu(pb_ref[...])
    w = gp.shape[1] // 2
    u = gp[:, :w]
    vn = _ln(gp[:, w:], mlg_ref[...], mlb_ref[...])
    tm = u.shape[0]
    first_group = lax.broadcasted_iota(jnp.int32, (1, LANES), 1) < HEAD_DIM
    rows = []
    for ci in range(tm // chunk):
        vc = vn[ci * chunk:(ci + 1) * chunk]
        cols = []
        for m in range(w // LANES):
            vp = vc[:, m * LANES:(m + 1) * LANES]
            stacked = jnp.concatenate([jnp.where(first_group, vp, 0.0), jnp.where(first_group, 0.0, vp)], axis=0)
            cols.append(_mm(ws_ref[m], stacked))
        rows.append(jnp.concatenate(cols, axis=1) + bs_ref[...])
    sv = jnp.concatenate(rows, axis=0)
    y_b = u * sv

    pg = pg_ref[...]
    d = pg.shape[1] // 2
    merged = _sigmoid(pg[:, :d]) * _mm(y_a, wbr_ref[0]) + _sigmoid(pg[:, d:]) * _mm(y_b, wbr_ref[1])
    mix = _mm(merged, wmix_ref[...])
    o_ref[...] = _ln(alpha * h0 + mix, l1g_ref[...], l1b_ref[...])


def _mixer(x2, yf, yb, g, bonus, pb, pg, lng, lnb, gng, gnb, bd, mlg, mlb, wscat, bsfull, wbr, wmix, l1g, l1b,
           alpha, chunk, tm):
    n, d = x2.shape
    w = g.shape[1]
    row = lambda c: pl.BlockSpec((tm, c), lambda i: (i, 0))
    return pl.pallas_call(
        functools.partial(_mixer_kernel, alpha, chunk),
        grid=(n // tm,),
        in_specs=[row(d), row(w), row(w), row(w), row(w), row(2 * w), row(2 * d),
                  _full((1, d)), _full((1, d)), _full((1, w)), _full((1, w)), _full((w, w)),
                  _full((1, w)), _full((1, w)), _full(wscat.shape), _full(bsfull.shape),
                  _full(wbr.shape), _full(wmix.shape), _full((1, d)), _full((1, d))],
        out_specs=row(d),
        out_shape=jax.ShapeDtypeStruct((n, d), F32),
        compiler_params=_cparams("parallel"),
        name="mixer",
    )(x2, yf, yb, g, bonus, pb, pg, lng, lnb, gng, gnb, bd, mlg, mlb, wscat, bsfull, wbr, wmix, l1g, l1b)


def _memkv_kernel(m_ref, g_ref, b_ref, w_ref, o_ref):
    o_ref[...] = _mm(_ln(m_ref[...], g_ref[...], b_ref[...]), w_ref[...]).astype(BF16)


def _memkv(mem2, g, b, wkv, tm):
    n, d = mem2.shape
    return pl.pallas_call(
        _memkv_kernel,
        grid=(n // tm,),
        in_specs=[pl.BlockSpec((tm, d), lambda i: (i, 0)), _full((1, d)), _full((1, d)), _full(wkv.shape)],
        out_specs=pl.BlockSpec((tm, wkv.shape[1]), lambda i: (i, 0)),
        out_shape=jax.ShapeDtypeStruct((n, wkv.shape[1]), BF16),
        compiler_params=_cparams("parallel"),
        name="memkv",
    )(mem2, g, b, wkv)


def _xattn_kernel(alpha, heads, h_ref, kv_ref, wq_ref, wo_ref, g_ref, b_ref, o_ref):
    h = h_ref[...]
    d = h.shape[1]
    hd = d // heads
    q = _mm(h, wq_ref[...])
    kv = kv_ref[...]
    scale = np.float32(hd ** -0.5)
    hs = range(heads)
    qb = q.astype(BF16)
    s = [_mm_nt(qb[:, i * hd:(i + 1) * hd], kv[:, i * hd:(i + 1) * hd]) * scale for i in hs]
    s = [x - jnp.max(x, axis=-1, keepdims=True) for x in s]
    e = [jnp.exp(x) for x in s]
    prob = [x / jnp.sum(x, axis=-1, keepdims=True) for x in e]
    o = [_mm(prob[i], kv[:, d + i * hd:d + (i + 1) * hd]) for i in hs]
    out = _mm(jnp.concatenate(o, axis=1), wo_ref[...])
    o_ref[...] = _ln(alpha * h + out, g_ref[...], b_ref[...])


def _xattn(h1, kv, wq, wo, g, b, alpha, heads, seq, mem_len, tm):
    n, d = h1.shape
    tps = seq // tm
    return pl.pallas_call(
        functools.partial(_xattn_kernel, alpha, heads),
        grid=(n // tm,),
        in_specs=[pl.BlockSpec((tm, d), lambda i: (i, 0)),
                  pl.BlockSpec((mem_len, 2 * d), lambda i: (i // tps, 0)),
                  _full(wq.shape), _full(wo.shape), _full((1, d)), _full((1, d))],
        out_specs=pl.BlockSpec((tm, d), lambda i: (i, 0)),
        out_shape=jax.ShapeDtypeStruct((n, d), F32),
        compiler_params=_cparams("parallel"),
        name="xattn",
    )(h1, kv, wq, wo, g, b)


def _extract_topk(s, pos, k):
    rank = jnp.full(s.shape, float(k), F32)
    vals = []
    for j in range(k):
        m = jnp.max(s, axis=0, keepdims=True)
        first = jnp.min(jnp.where(s == m, pos, np.float32(np.inf)), axis=0, keepdims=True)
        sel = pos == first
        rank = jnp.where(sel, float(j), rank)
        s = jnp.where(sel, -jnp.inf, s)
        vals.append(m)
    return vals, rank


def _bf16_pair_word(x):
    hi = pltpu.bitcast(x.astype(BF16).astype(F32), jnp.int32) & jnp.int32(-65536)
    return hi | lax.shift_right_logical(hi, 16)


def _candidate_layout(k):
    pieces = [("row", 0, 0, 16), ("row", 1, 0, 8), ("col", 0, 0, 16), ("col", 1, 0, 8),
              ("col", 2, 0, 8), ("col", 3, 0, 8), ("col", 4, 0, 8)]
    pos, seen = [], set()
    for kind, fixed, start, n in pieces:
        for o in range(start, start + n):
            i, j = (fixed, o) if kind == "row" else (o, fixed)
            ok = (i + 1) * (j + 1) <= k and (i, j) not in seen
            if ok:
                seen.add((i, j))
            pos.append(float(i * k + j) if ok else -1.0)
    assert len(seen) == sum(k // (i + 1) for i in range(k))
    return pieces, np.asarray(pos, np.float32)


def _sort16_comparators():
    def merge(lo, hi, r):
        step = r * 2
        if step < hi - lo:
            yield from merge(lo, hi, step)
            yield from merge(lo + r, hi, step)
            yield from ((i, i + r) for i in range(lo + r, hi - r, step))
        else:
            yield (lo, lo + r)

    def sort(lo, hi):
        if hi - lo >= 1:
            mid = lo + (hi - lo) // 2
            yield from sort(lo, mid)
            yield from sort(mid + 1, hi)
            yield from merge(lo, hi, 1)

    return tuple(sort(0, 15))


def _top16_sorted(s):
    def exchange(v, i, j):
        v[i], v[j] = jnp.maximum(v[i], v[j]), jnp.minimum(v[i], v[j])

    v = [s[8 * r:8 * r + 8] for r in range(16)]
    for i, j in _sort16_comparators():
        exchange(v, i, j)
    for shift in (4, 2, 1):
        w = [pltpu.roll(x, shift, 0) for x in v]
        v = [jnp.maximum(v[i], w[15 - i]) for i in range(16)]
        for stride in (8, 4, 2, 1):
            for i in range(16):
                if not i & stride:
                    exchange(v, i, i + stride)
    return v


def _route_kernel(heads, h_ref, wq_ref, sk_ref, cpos_ref, cnt_ref, e1_ref, rank2_ref, e2_ref, q_scr):
    h = h_ref[...]
    tt = h.shape[0]
    q_scr[...] = _mm(h, wq_ref[...])
    k = TOPK
    pieces, _ = _candidate_layout(k)
    cpos = cpos_ref[...]
    cvalid = cpos >= 0.0
    cpos_inf = jnp.where(cvalid, cpos, np.float32(np.inf))
    key_pos = lax.broadcasted_iota(jnp.int32, (N_KEYS, tt), 0).astype(F32)
    row16 = lax.broadcasted_iota(jnp.int32, (k, 1), 0)

    def scores(hd):
        off = pl.multiple_of(hd * (2 * LANES), 2 * LANES)
        s1 = _mm_nt_hi(sk_ref[0], q_scr[:, pl.ds(off, LANES)])
        s2 = _mm_nt_hi(sk_ref[1], q_scr[:, pl.ds(off + LANES, LANES)])
        return s1, s2

    def candidates(top1, top2):
        t1 = jnp.concatenate(top1, axis=0)
        t2 = jnp.concatenate(top2, axis=0)
        parts = []
        for kind, fixed, start, n in pieces:
            if kind == "row":
                parts.append(top1[fixed] + t2[start:start + n])
            else:
                parts.append(t1[start:start + n] + top2[fixed])
        return t1, jnp.where(cvalid, jnp.concatenate(parts, axis=0), -jnp.inf)

    def pair_selection_by_value(top1, top2):
        t1, cand = candidates(top1, top2)
        chosen = jnp.zeros_like(cand)
        best = []
        for j in range(k):
            m = jnp.max(cand, axis=0, keepdims=True)
            sel = cand == m
            chosen = jnp.where(sel, 1.0, chosen)
            cand = jnp.where(sel, -jnp.inf, cand)
            best.append(m)
        inf = np.float32(np.inf)
        theta = jnp.full((k, tt), inf, F32)
        r0 = 0
        for kind, fixed, start, n in pieces:
            ch = chosen[r0:r0 + n] > 0.0
            r0 += n
            if kind == "row":
                val = jnp.where(ch, top1[fixed], inf)
                if n < k:
                    val = jnp.concatenate([val, jnp.full((k - n, tt), inf, F32)], axis=0)
                theta = jnp.minimum(theta, val)
            else:
                val = jnp.min(jnp.where(ch, t1[start:start + n], inf), axis=0, keepdims=True)
                theta = jnp.where(row16 == fixed, jnp.minimum(theta, val), theta)
        return best, jnp.sum(chosen, axis=0, keepdims=True), theta

    def pair_selection(top1, top2):
        _, cand = candidates(top1, top2)
        best, crank = _extract_topk(cand, cpos_inf, k)
        chosen = jnp.where(crank < float(k), 1.0, 0.0)
        n16 = jnp.zeros((k, tt), F32)
        r0 = 0
        for kind, fixed, start, n in pieces:
            c = chosen[r0:r0 + n]
            r0 += n
            if kind == "row":
                n16 = n16 + jnp.where(row16 == fixed, jnp.sum(c, axis=0, keepdims=True), 0.0)
            elif n == k:
                n16 = n16 + c
            else:
                n16 = n16 + jnp.concatenate([c, jnp.zeros((k - n, tt), F32)], axis=0)
        return best, n16

    def emit(hd, s1, s2, max1, max2, best, cnt_dense, rank2):
        z = jnp.zeros_like(best[0])
        for j in range(k):
            z = z + jnp.exp(best[j] - best[0])
        cnt_ref[hd] = _bf16_pair_word(cnt_dense)
        e1_ref[hd] = jnp.exp(s1 - max1) / z
        rank2_ref[hd] = rank2.astype(BF16)
        e2_ref[hd] = jnp.exp(s2 - max2).astype(BF16)

    def head_by_sorting(hd, bad):
        s1, s2 = scores(hd)
        v1 = _top16_sorted(s1)
        v2 = _top16_sorted(s2)
        top1 = [x[0:1] for x in v1]
        top2 = [x[0:1] for x in v2]
        best, taken, theta = pair_selection_by_value(top1, top2)
        rank2 = jnp.zeros_like(s2)
        cnt_dense = jnp.zeros_like(s1)
        for i in range(k):
            rank2 = rank2 + jnp.where(top2[i] > s2, 1.0, 0.0)
            cnt_dense = cnt_dense + jnp.where(s1 >= theta[i:i + 1], 1.0, 0.0)
        in1 = jnp.sum(jnp.where(s1 >= top1[k - 1], 1.0, 0.0), axis=0, keepdims=True)
        in2 = jnp.sum(jnp.where(rank2 < float(k), 1.0, 0.0), axis=0, keepdims=True)
        ties = (jnp.where(in1 == float(k), 0.0, 1.0) + jnp.where(in2 == float(k), 0.0, 1.0)
                + jnp.where(taken == float(k), 0.0, 1.0))
        for i in range(k - 1):
            ties = ties + jnp.where(top1[i] > top1[i + 1], 0.0, 1.0) + jnp.where(top2[i] > top2[i + 1], 0.0, 1.0)
        emit(hd, s1, s2, top1[0], top2[0], best, cnt_dense, rank2)
        return bad + ties

    def head_exact(hd, carry):
        s1, s2 = scores(hd)
        top1, rank1 = _extract_topk(s1, key_pos, k)
        top2, rank2 = _extract_topk(s2, key_pos, k)
        best, n16 = pair_selection(top1, top2)
        cnt_dense = jnp.zeros_like(s1)
        for i in range(k):
            cnt_dense = cnt_dense + jnp.where(rank1 == float(i), n16[i:i + 1], 0.0)
        emit(hd, s1, s2, top1[0], top2[0], best, cnt_dense, rank2)
        return carry

    bad = lax.fori_loop(0, heads, head_by_sorting, jnp.zeros((1, tt), F32))

    @pl.when(jnp.sum(bad) > 0.0)
    def _():
        lax.fori_loop(0, heads, head_exact, 0)


def _route(h2, wq, sk, heads, tt):
    n, d = h2.shape
    _, pos = _candidate_layout(TOPK)
    cpos = jnp.asarray(np.broadcast_to(pos[:, None], (pos.shape[0], tt)).copy())
    blk = pl.BlockSpec((heads, N_KEYS, tt), lambda i: (0, 0, i))
    return pl.pallas_call(
        functools.partial(_route_kernel, heads),
        grid=(n // tt,),
        in_specs=[pl.BlockSpec((tt, d), lambda i: (i, 0)), _full(wq.shape), _full(sk.shape), _full(cpos.shape)],
        out_specs=[blk] * 4,
        out_shape=[jax.ShapeDtypeStruct((heads, N_KEYS, n), jnp.int32), jax.ShapeDtypeStruct((heads, N_KEYS, n), F32)]
                  + [jax.ShapeDtypeStruct((heads, N_KEYS, n), BF16)] * 2,
        scratch_shapes=[pltpu.VMEM((tt, wq.shape[1]), F32)],
        compiler_params=_cparams("parallel"),
        name="route",
    )(h2, wq.astype(BF16), sk, cpos)


def _peer_kernel(alpha, heads, n_pairs, h_ref, ua_ref, ub_ref, vtp_ref, vtc_ref,
                 cnt_ref, e1_ref, rank2_ref, e2_ref, g_ref, b_ref, o_ref, acc_scr, wa_scr, wb_scr, hb_scr):
    p = pl.program_id(1)
    te = ua_ref.shape[0]
    tt = h_ref.shape[0]
    blocks = te // N_KEYS
    sub = 16
    grp = N_KEYS // sub
    dot = functools.partial(jnp.dot, preferred_element_type=F32)

    @pl.when(p == 0)
    def _():
        acc_scr[...] = jnp.zeros_like(acc_scr)
        wb_scr[...] = jnp.zeros_like(wb_scr)
        hb_scr[...] = h_ref[...].astype(BF16)

    def gate_tile(u_ref, tile, w_out):
        act_all = lax.dot_general(u_ref[...], hb_scr[...], (((1,), (1,)), ((), ())),
                                  preferred_element_type=F32)
        for il in range(blocks):
            ig = tile * blocks + il
            act = act_all[il * N_KEYS:(il + 1) * N_KEYS]
            gate = jnp.zeros((grp, sub, tt), BF16)
            for hd in range(heads):
                cnt = pltpu.bitcast(jnp.broadcast_to(cnt_ref[hd, pl.ds(ig, 1), :], (sub // 2, tt)), BF16)
                e1 = jnp.broadcast_to(e1_ref[hd, pl.ds(ig, 1), :], (sub, tt)).astype(BF16)
                rank2 = rank2_ref[hd].reshape(grp, sub, tt)
                e2 = e2_ref[hd].reshape(grp, sub, tt)
                gate = gate + jnp.where(rank2 < cnt[None], e2, jnp.zeros_like(e2)) * e1[None]
            w = gate * _gelu(act).astype(BF16).reshape(grp, sub, tt)
            w_out[il * N_KEYS:(il + 1) * N_KEYS, :] = w.reshape(N_KEYS, tt)

    @pl.when(p < n_pairs)
    def _():
        acc_scr[...] += dot(vtp_ref[...], wb_scr[...])
        gate_tile(ua_ref, 2 * p, wa_scr)
        acc_scr[...] += dot(vtc_ref[...], wa_scr[...])
        gate_tile(ub_ref, 2 * p + 1, wb_scr)

    @pl.when(p == n_pairs)
    def _():
        y = jnp.transpose(acc_scr[...] + dot(vtp_ref[...], wb_scr[...]))
        o_ref[...] = _ln(alpha * h_ref[...] + y, g_ref[...], b_ref[...])


def _peer(h2, u_bf, vt_bf, cnt, e1, rank2, e2, g, b, alpha, heads, tt, te):
    n, d = h2.shape
    ne = u_bf.shape[0] // te
    n_pairs = ne // 2
    assert ne == 2 * n_pairs and ne >= 2
    blk = pl.BlockSpec((heads, N_KEYS, tt), lambda i, p: (0, 0, i))
    return pl.pallas_call(
        functools.partial(_peer_kernel, alpha, heads, n_pairs),
        grid=(n // tt, n_pairs + 1),
        in_specs=[pl.BlockSpec((tt, d), lambda i, p: (i, 0)),
                  pl.BlockSpec((te, d), lambda i, p: (jnp.minimum(2 * p, ne - 2), 0)),
                  pl.BlockSpec((te, d), lambda i, p: (jnp.minimum(2 * p + 1, ne - 1), 0)),
                  pl.BlockSpec((d, te), lambda i, p: (0, jnp.maximum(2 * p - 1, 1))),
                  pl.BlockSpec((d, te), lambda i, p: (0, jnp.minimum(2 * p, ne - 2))),
                  blk, blk, blk, blk, _full((1, d)), _full((1, d))],
        out_specs=pl.BlockSpec((tt, d), lambda i, p: (i, 0)),
        out_shape=jax.ShapeDtypeStruct((n, d), F32),
        scratch_shapes=[pltpu.VMEM((d, tt), F32), pltpu.VMEM((te, tt), BF16), pltpu.VMEM((te, tt), BF16),
                        pltpu.VMEM((tt, d), BF16)],
        compiler_params=_cparams("parallel", "arbitrary"),
        name="peer",
    )(h2, u_bf, u_bf, vt_bf, vt_bf, cnt, e1, rank2, e2, g, b)


def _block_diag2(w):
    z = jnp.zeros_like(w[0])
    return jnp.concatenate([jnp.concatenate([w[0], z], axis=1), jnp.concatenate([z, w[1]], axis=1)], axis=0)


def _layer(h_in, mem2, batch, seq, mem_len, lng, lnb, first, p):
    del first
    n, d = h_in.shape
    (w_in, rwkv_mu, rwkv_w0, rwkv_w2, rwkv_a0, rwkv_a2, rwkv_g2, rwkv_k_k, rwkv_k_a, rwkv_r_k,
     rwkv_gn_g, rwkv_gn_b, gmlp_ln_g, gmlp_ln_b, gmlp_w_s, gmlp_b_s, w_branch, w_mix_out, ln1_g, ln1_b,
     mem_ln_g, mem_ln_b, xattn_w_q, xattn_w_kv, xattn_w_o, ln2_g, ln2_b,
     peer_w_query, peer_sub_keys, peer_u, peer_v, ln3_g, ln3_b, alpha) = p
    width = rwkv_k_k.shape[0]
    rw_cols = rwkv_mu.shape[0]
    gm_cols = 2 * gmlp_ln_g.shape[0]
    chunk = gmlp_w_s.shape[1]
    row1 = lambda t: t.reshape(1, -1)

    tm = 256
    wa = w_in[:, :rw_cols].astype(BF16)
    wb = w_in[:, rw_cols:rw_cols + gm_cols].astype(BF16)
    wg = w_in[:, rw_cols + gm_cols:].astype(BF16)
    pa, pb, pg = _inproj(h_in, row1(lng), row1(lnb), wa, wb, wg, tm)

    heads_r = width // HEAD_DIM
    bd = jnp.asarray(np.kron(np.eye(heads_r, dtype=np.float32), np.ones((HEAD_DIM, HEAD_DIM), np.float32)), BF16)
    r, v, na, lw, kd, bv, g, bonus = _rwkv_pre(
        pa, seq, row1(rwkv_mu), row1(rwkv_w0), _hilo(_block_diag2(rwkv_w2)), row1(rwkv_a0),
        _hilo(_block_diag2(rwkv_a2)), _hilo(rwkv_g2), row1(rwkv_k_k), row1(rwkv_k_a), row1(rwkv_r_k), bd, tm)
    yf, yb = _scan(r, v, na, lw, kd, bv, batch, seq)

    groups = gmlp_w_s.shape[0]
    wscat = jnp.concatenate([gmlp_w_s[0::2], gmlp_w_s[1::2]], axis=2).astype(BF16)
    bsfull = jnp.repeat(gmlp_b_s.T, HEAD_DIM, axis=1)
    assert groups * HEAD_DIM == width
    h1 = _mixer(h_in, yf, yb, g, bonus, pb, pg, row1(lng), row1(lnb), row1(rwkv_gn_g), row1(rwkv_gn_b), bd,
                row1(gmlp_ln_g), row1(gmlp_ln_b), wscat, bsfull, w_branch.astype(BF16), w_mix_out.astype(BF16),
                row1(ln1_g), row1(ln1_b), alpha, chunk, tm)

    kv = _memkv(mem2, row1(mem_ln_g), row1(mem_ln_b), xattn_w_kv.astype(BF16), mem_len)
    xheads = 4
    h2 = _xattn(h1, kv, xattn_w_q.astype(BF16), xattn_w_o.astype(BF16), row1(ln2_g), row1(ln2_b),
                alpha, xheads, seq, mem_len, tm)

    pheads = peer_w_query.shape[1] // (2 * LANES)
    cnt, e1, rank2, e2 = _route(h2, peer_w_query, peer_sub_keys, pheads, 256)
    h3 = _peer(h2, peer_u.astype(BF16), peer_v.T.astype(BF16), cnt, e1, rank2, e2, row1(ln3_g), row1(ln3_b),
               alpha, pheads, 512, 1024)
    return h3


def kernel(x, mem, ln_emb_g, ln_emb_b, w_in, rwkv_mu, rwkv_w0, rwkv_w2, rwkv_a0, rwkv_a2, rwkv_g2, rwkv_k_k, rwkv_k_a, rwkv_r_k, rwkv_gn_g, rwkv_gn_b, gmlp_ln_g, gmlp_ln_b, gmlp_w_s, gmlp_b_s, w_branch, w_mix_out, ln1_g, ln1_b, mem_ln_g, mem_ln_b, xattn_w_q, xattn_w_kv, xattn_w_o, ln2_g, ln2_b, peer_w_query, peer_sub_keys, peer_u, peer_v, ln3_g, ln3_b):
    batch, seq, d = x.shape
    mem_len = mem.shape[1]
    depth = w_in.shape[0]
    assert depth == 1, "the layer pipeline applies the embedding LN inside the first layer only"
    alpha = float((2.0 * depth) ** 0.25)
    x2 = x.reshape(batch * seq, d)
    mem2 = mem.reshape(batch * mem_len, d)
    flat = lambda t: t.reshape(-1)
    p = (w_in[0], rwkv_mu[0], flat(rwkv_w0[0]), rwkv_w2[0], flat(rwkv_a0[0]), rwkv_a2[0], rwkv_g2[0],
         rwkv_k_k[0], rwkv_k_a[0], flat(rwkv_r_k[0]), rwkv_gn_g[0], rwkv_gn_b[0], gmlp_ln_g[0], gmlp_ln_b[0],
         gmlp_w_s[0], gmlp_b_s[0], w_branch[0], w_mix_out[0], ln1_g[0], ln1_b[0], mem_ln_g[0], mem_ln_b[0],
         xattn_w_q[0], xattn_w_kv[0], xattn_w_o[0], ln2_g[0], ln2_b[0], peer_w_query[0], peer_sub_keys[0],
         peer_u[0], peer_v[0], ln3_g[0], ln3_b[0], alpha)
    h = _layer(x2, mem2, batch, seq, mem_len, ln_emb_g, ln_emb_b, True, p)
    return h.reshape(batch, seq, d)
```

```python
import functools

import numpy as np
import jax
import jax.numpy as jnp
from jax import lax
from jax.experimental import pallas as pl
from jax.experimental.pallas import tpu as pltpu

F32 = jnp.float32
BF16 = jnp.bfloat16
HIGHEST = lax.Precision.HIGHEST

LN_EPS = 1e-5
GN_EPS = 64e-5
LANES = 128
HEAD_DIM = 64
SCAN_CHUNK = 64
TOPK = 16
N_KEYS = 128
VMEM_LIMIT = 56 * 1024 * 1024


def _cparams(*sem):
    return pltpu.CompilerParams(dimension_semantics=sem, vmem_limit_bytes=VMEM_LIMIT)


def _ln(x, g, b, eps=LN_EPS):
    mu = jnp.mean(x, axis=-1, keepdims=True)
    xc = x - mu
    var = jnp.mean(xc * xc, axis=-1, keepdims=True)
    return xc * lax.rsqrt(var + eps) * g + b


def _mm(a, b):
    return jnp.dot(a.astype(BF16), b.astype(BF16), preferred_element_type=F32)


def _mm_nt(a, b):
    return lax.dot_general(a, b, (((1,), (1,)), ((), ())), preferred_element_type=F32)


def _mm_nt_hi(a, b):
    return lax.dot_general(a, b, (((1,), (1,)), ((), ())), precision=HIGHEST,
                           preferred_element_type=F32)


def _split_bf16(x):
    hi = x.astype(BF16)
    return hi, (x - hi.astype(F32)).astype(BF16)


def _hilo(w):
    hi, lo = _split_bf16(w)
    return jnp.stack([hi, lo])


def _mm_x3(x, w_ref):
    hi, lo = _split_bf16(x)
    w_hi = w_ref[0]
    return (jnp.dot(hi, w_hi, preferred_element_type=F32) + jnp.dot(hi, w_ref[1], preferred_element_type=F32)
            + jnp.dot(lo, w_hi, preferred_element_type=F32))


def _segsum(x, ones):
    hi, lo = _split_bf16(x)
    return jnp.dot(hi, ones, preferred_element_type=F32) + jnp.dot(lo, ones, preferred_element_type=F32)


def _gelu(x):
    return 0.5 * x * (1.0 + lax.erf(x * np.float32(1.0 / np.sqrt(2.0))))


def _sigmoid(x):
    return 1.0 / (1.0 + jnp.exp(-x))


def _full(shape):
    nd = len(shape)
    return pl.BlockSpec(shape, lambda *_: (0,) * nd)


def _inproj_kernel(x_ref, g_ref, b_ref, wa_ref, wb_ref, wg_ref, pa_ref, pb_ref, pg_ref):
    h = _ln(x_ref[...], g_ref[...], b_ref[...]).astype(BF16)
    pa_ref[...] = jnp.dot(h, wa_ref[...], preferred_element_type=F32)
    pb_ref[...] = jnp.dot(h, wb_ref[...], preferred_element_type=F32)
    pg_ref[...] = jnp.dot(h, wg_ref[...], preferred_element_type=F32)


def _inproj(x2, g, b, wa, wb, wg, tm):
    n, d = x2.shape
    ca, cb, cg = wa.shape[1], wb.shape[1], wg.shape[1]
    row = lambda c: pl.BlockSpec((tm, c), lambda i: (i, 0))
    return pl.pallas_call(
        _inproj_kernel,
        grid=(n // tm,),
        in_specs=[row(d), _full((1, d)), _full((1, d)), _full((d, ca)), _full((d, cb)), _full((d, cg))],
        out_specs=[row(ca), row(cb), row(cg)],
        out_shape=[jax.ShapeDtypeStruct((n, c), F32) for c in (ca, cb, cg)],
        compiler_params=_cparams("parallel"),
        name="inproj",
    )(x2, g, b, wa, wb, wg)


def _rwkv_pre_kernel(tiles_per_seq, width,
                     p_ref, prev_ref, next_ref, mu_ref, w0_ref, w2_ref, a0_ref, a2_ref, g2_ref,
                     kk_ref, ka_ref, rk_ref, bd_ref,
                     r_out, v_out, na_out, lw_out, kd_out, bv_out, g_out, bonus_out):
    i = pl.program_id(0)
    pos = i % tiles_per_seq
    p = p_ref[...]
    tm = p.shape[0]
    row = lax.broadcasted_iota(jnp.int32, (tm, 1), 0)
    prev_row = jnp.where(pos == 0, 0.0, prev_ref[7:8, :])
    next_row = jnp.where(pos == tiles_per_seq - 1, 0.0, next_ref[0:1, :])
    prev = jnp.where(row == 0, prev_row, pltpu.roll(p, 1, 0))
    nxt = jnp.where(row == tm - 1, next_row, pltpu.roll(p, tm - 1, 0))
    ps = p + mu_ref[...] * (0.5 * (prev + nxt) - p)

    w = width
    r = ps[:, 0:w]
    k = ps[:, w:2 * w]
    v = ps[:, 2 * w:3 * w]
    wd = ps[:, 3 * w:3 * w + LANES]
    ad = ps[:, 3 * w + LANES:3 * w + 2 * LANES]
    gd = ps[:, 3 * w + 2 * LANES:3 * w + 3 * LANES]

    wz = w0_ref[...] + _mm_x3(jnp.tanh(wd), w2_ref)
    w_log = -(jnp.maximum(-wz, 0.0) + jnp.log(1.0 + jnp.exp(-jnp.abs(wz)))) - 0.5
    log_decay = -jnp.exp(w_log)
    a = _sigmoid(a0_ref[...] + _mm_x3(ad, a2_ref))
    g = _mm_x3(_sigmoid(gd), g2_ref)

    bd = bd_ref[...]
    kk = k * kk_ref[...]
    nrm = jnp.sqrt(_segsum(kk * kk, bd))
    kk = kk / jnp.maximum(nrm, 1e-12)

    ka = ka_ref[...]
    k0 = k * (1.0 + (a[:, 0:w] - 1.0) * ka)
    k1 = k * (1.0 + (a[:, w:2 * w] - 1.0) * ka)
    rk = _segsum(r * (k0 + k1) * rk_ref[...], bd)

    r_out[...] = r
    v_out[...] = v
    na_out[...] = -kk
    lw_out[0] = log_decay[:, 0:w]
    lw_out[1] = log_decay[:, w:2 * w]
    kd_out[0] = k0
    kd_out[1] = k1
    bv_out[0] = kk * a[:, 0:w]
    bv_out[1] = kk * a[:, w:2 * w]
    g_out[...] = g
    bonus_out[...] = rk * v


def _rwkv_pre(pa, seq, mu, w0, w2bd, a0, a2bd, g2, k_k, k_a, r_k, bd, tm):
    n, ca = pa.shape
    w = bd.shape[0]
    tps = seq // tm
    hb = tm // 8
    nb8 = n // 8
    row = lambda c: pl.BlockSpec((tm, c), lambda i: (i, 0))
    row2 = pl.BlockSpec((2, tm, w), lambda i: (0, i, 0))
    outs = [jax.ShapeDtypeStruct((n, w), F32)] * 3 + [jax.ShapeDtypeStruct((2, n, w), F32)] * 3 + \
           [jax.ShapeDtypeStruct((n, w), F32)] * 2
    return pl.pallas_call(
        functools.partial(_rwkv_pre_kernel, tps, w),
        grid=(n // tm,),
        in_specs=[row(ca),
                  pl.BlockSpec((8, ca), lambda i: (jnp.maximum(i * hb - 1, 0), 0)),
                  pl.BlockSpec((8, ca), lambda i: (jnp.minimum((i + 1) * hb, nb8 - 1), 0)),
                  _full((1, ca)), _full((1, 2 * w)), _full(w2bd.shape), _full((1, 2 * w)),
                  _full(a2bd.shape), _full(g2.shape), _full((1, w)), _full((1, w)), _full((1, w)),
                  _full((w, w))],
        out_specs=[row(w), row(w), row(w), row2, row2, row2, row(w), row(w)],
        out_shape=outs,
        compiler_params=_cparams("parallel"),
        name="rwkv_pre",
    )(pa, pa, pa, mu, w0, w2bd, a0, a2bd, g2, k_k, k_a, r_k, bd)


def _scan_kernel(rf_ref, vf_ref, af_ref, rb_ref, vb_ref, ab_ref, lwf_ref, kf_ref, bf_ref, lwb_ref, kb_ref, bb_ref,
                 ms_ref, mi_ref, lc_ref, yf_ref, yb_ref, s_scr):
    c = SCAN_CHUNK
    h2 = 2 * c

    @pl.when(pl.program_id(1) == 0)
    def _():
        s_scr[...] = jnp.zeros_like(s_scr)

    first_head = lax.broadcasted_iota(jnp.int32, (1, LANES), 1) < HEAD_DIM

    def stack(x):
        return jnp.concatenate([jnp.where(first_head, x, 0.0), jnp.where(first_head, 0.0, x)], axis=0)

    dot = functools.partial(jnp.dot, preferred_element_type=F32)
    n_groups = lwf_ref.shape[1] // LANES
    sls = [slice(m * LANES, (m + 1) * LANES) for m in range(n_groups)]

    ar, bk, v_s, bk_e, p_end, strict, incl = [], [], [], [], [], [], []
    dirs = ((rf_ref, vf_ref, af_ref, lwf_ref, kf_ref, bf_ref), (rb_ref, vb_ref, ab_ref, lwb_ref, kb_ref, bb_ref))
    for d, (r_ref, v_ref, a_ref, lw_ref, k_ref, b_ref) in enumerate(dirs):
        lw = lw_ref[...]
        l1 = lw.astype(BF16)
        rest = lw - l1.astype(F32)
        l2 = rest.astype(BF16)
        l3 = (rest - l2.astype(F32)).astype(BF16)
        lc = lc_ref[d]
        cum = dot(lc, l1) + dot(lc, l2) + dot(lc, l3)
        tot = jnp.sum(lw, axis=0, keepdims=True)
        e_neg = jnp.exp(-cum)
        e_end = jnp.exp(tot - cum)
        a_t = a_ref[...] * jnp.exp(cum - lw)
        r_t = r_ref[...] * jnp.exp(cum)
        b_in = b_ref[...]
        k_in = k_ref[...]
        b_t = b_in * e_neg
        k_t = k_in * e_neg
        b_e = b_in * e_end
        k_e = k_in * e_end
        v_in = v_ref[...]
        pe = jnp.exp(tot)
        sd = ms_ref[d] > 0.0
        ic = mi_ref[d] > 0.0
        for s in sls:
            ar.append(jnp.concatenate([stack(a_t[:, s]), stack(r_t[:, s])], axis=0).astype(BF16))
            bk.append(jnp.concatenate([stack(b_t[:, s]), stack(k_t[:, s])], axis=0).astype(BF16))
            bk_e.append(jnp.concatenate([stack(b_e[:, s]), stack(k_e[:, s])], axis=0).astype(BF16))
            v_s.append(stack(v_in[:, s]))
            p_end.append(pe[:, s])
            strict.append(sd)
            incl.append(ic)
    gs = range(2 * n_groups)
    v_b = [x.astype(BF16) for x in v_s]
    state = [s_scr[m] for m in gs]

    l_all = [_mm_nt(ar[m], bk[m]) for m in gs]
    from_state = [_mm_nt(ar[m], state[m].astype(BF16)) for m in gs]
    l_ab = [jnp.where(strict[m], l_all[m][:h2, :h2], 0.0).astype(BF16) for m in gs]
    l_k = [jnp.concatenate([jnp.where(strict[m], l_all[m][:h2, h2:], 0.0),
                            jnp.where(incl[m], l_all[m][h2:, h2:], 0.0)], axis=0).astype(BF16) for m in gs]
    l_rb = [jnp.where(incl[m], l_all[m][h2:, :h2], 0.0).astype(BF16) for m in gs]
    from_v = [dot(l_k[m], v_b[m]) for m in gs]

    x = [from_state[m][:h2] + from_v[m][:h2] for m in gs]
    lp = l_ab
    x = [x[m] + dot(lp[m], x[m].astype(BF16)) for m in gs]
    span = 2
    while span < c:
        lp = [dot(lp[m], lp[m]).astype(BF16) for m in gs]
        x = [x[m] + dot(lp[m], x[m].astype(BF16)) for m in gs]
        span *= 2
    u_s = x

    y_s = [from_state[m][h2:] + from_v[m][h2:] + dot(l_rb[m], u_s[m].astype(BF16)) for m in gs]
    uv_t = [jnp.transpose(jnp.concatenate([u_s[m], v_s[m]], axis=0)).astype(BF16) for m in gs]
    new_state = [state[m] * p_end[m] + dot(uv_t[m], bk_e[m]) for m in gs]
    for m in gs:
        y_ref = yf_ref if m < n_groups else yb_ref
        y_ref[:, sls[m % n_groups]] = y_s[m][:c] + y_s[m][c:]
        s_scr[m] = new_state[m]


def _scan(r, v, na, lw, kd, bv, batch, seq):
    n, w = r.shape
    c = SCAN_CHUNK
    nc = seq // c
    t = np.arange(c)
    before = [(t[None, :] < t[:, None]), (t[None, :] > t[:, None])]
    eye = np.eye(c, dtype=bool)
    tile2 = lambda m: np.tile(m, (2, 2))
    ms = jnp.asarray(np.stack([tile2(m) for m in before]).astype(np.float32))
    mi = jnp.asarray(np.stack([tile2(m | eye) for m in before]).astype(np.float32))
    lc = jnp.asarray(np.stack([(m | eye) for m in before]).astype(np.float32), BF16)

    fwd = pl.BlockSpec((c, w), lambda b, ci: (b * nc + ci, 0))
    bwd = pl.BlockSpec((c, w), lambda b, ci: (b * nc + nc - 1 - ci, 0))
    fwd_d = pl.BlockSpec((None, c, w), lambda b, ci: (0, b * nc + ci, 0))
    bwd_d = pl.BlockSpec((None, c, w), lambda b, ci: (1, b * nc + nc - 1 - ci, 0))
    return pl.pallas_call(
        _scan_kernel,
        grid=(batch, nc),
        in_specs=[fwd, fwd, fwd, bwd, bwd, bwd, fwd_d, fwd_d, fwd_d, bwd_d, bwd_d, bwd_d,
                  _full(ms.shape), _full(mi.shape), _full(lc.shape)],
        out_specs=[fwd, bwd],
        out_shape=[jax.ShapeDtypeStruct((n, w), F32)] * 2,
        scratch_shapes=[pltpu.VMEM((2 * (w // LANES), LANES, LANES), F32)],
        compiler_params=_cparams("parallel", "arbitrary"),
        name="scan",
    )(r, v, na, r, v, na, lw, kd, bv, lw, kd, bv, ms, mi, lc)


def _mixer_kernel(alpha, chunk,
                  x_ref, yf_ref, yb_ref, g_ref, bonus_ref, pb_ref, pg_ref,
                  lng_ref, lnb_ref, gng_ref, gnb_ref, bd_ref, mlg_ref, mlb_ref, ws_ref, bs_ref,
                  wbr_ref, wmix_ref, l1g_ref, l1b_ref, o_ref):
    h0 = _ln(x_ref[...], lng_ref[...], lnb_ref[...])
    bd = bd_ref[...]
    inv = np.float32(1.0 / HEAD_DIM)

    y = yf_ref[...] + yb_ref[...]
    mu = _segsum(y, bd) * inv
    yc = y - mu
    var = _segsum(yc * yc, bd) * inv
    y_a = (yc * lax.rsqrt(var + GN_EPS) * gng_ref[...] + gnb_ref[...] + bonus_ref[...]) * g_ref[...]

    gp = _gelu(pb_ref[...])
    w = gp.shape[1] // 2
    u = gp[:, :w]
    vn = _ln(gp[:, w:], mlg_ref[...], mlb_ref[...])
    tm = u.shape[0]
    first_group = lax.broadcasted_iota(jnp.int32, (1, LANES), 1) < HEAD_DIM
    rows = []
    for ci in range(tm // chunk):
        vc = vn[ci * chunk:(ci + 1) * chunk]
        cols = []
        for m in range(w // LANES):
            vp = vc[:, m * LANES:(m + 1) * LANES]
            stacked = jnp.concatenate([jnp.where(first_group, vp, 0.0), jnp.where(first_group, 0.0, vp)], axis=0)
            cols.append(_mm(ws_ref[m], stacked))
        rows.append(jnp.concatenate(cols, axis=1) + bs_ref[...])
    sv = jnp.concatenate(rows, axis=0)
    y_b = u * sv

    pg = pg_ref[...]
    d = pg.shape[1] // 2
    merged = _sigmoid(pg[:, :d]) * _mm(y_a, wbr_ref[0]) + _sigmoid(pg[:, d:]) * _mm(y_b, wbr_ref[1])
    mix = _mm(merged, wmix_ref[...])
    o_ref[...] = _ln(alpha * h0 + mix, l1g_ref[...], l1b_ref[...])


def _mixer(x2, yf, yb, g, bonus, pb, pg, lng, lnb, gng, gnb, bd, mlg, mlb, wscat, bsfull, wbr, wmix, l1g, l1b,
           alpha, chunk, tm):
    n, d = x2.shape
    w = g.shape[1]
    row = lambda c: pl.BlockSpec((tm, c), lambda i: (i, 0))
    return pl.pallas_call(
        functools.partial(_mixer_kernel, alpha, chunk),
        grid=(n // tm,),
        in_specs=[row(d), row(w), row(w), row(w), row(w), row(2 * w), row(2 * d),
                  _full((1, d)), _full((1, d)), _full((1, w)), _full((1, w)), _full((w, w)),
                  _full((1, w)), _full((1, w)), _full(wscat.shape), _full(bsfull.shape),
                  _full(wbr.shape), _full(wmix.shape), _full((1, d)), _full((1, d))],
        out_specs=row(d),
        out_shape=jax.ShapeDtypeStruct((n, d), F32),
        compiler_params=_cparams("parallel"),
        name="mixer",
    )(x2, yf, yb, g, bonus, pb, pg, lng, lnb, gng, gnb, bd, mlg, mlb, wscat, bsfull, wbr, wmix, l1g, l1b)


def _memkv_kernel(m_ref, g_ref, b_ref, w_ref, o_ref):
    o_ref[...] = _mm(_ln(m_ref[...], g_ref[...], b_ref[...]), w_ref[...]).astype(BF16)


def _memkv(mem2, g, b, wkv, tm):
    n, d = mem2.shape
    return pl.pallas_call(
        _memkv_kernel,
        grid=(n // tm,),
        in_specs=[pl.BlockSpec((tm, d), lambda i: (i, 0)), _full((1, d)), _full((1, d)), _full(wkv.shape)],
        out_specs=pl.BlockSpec((tm, wkv.shape[1]), lambda i: (i, 0)),
        out_shape=jax.ShapeDtypeStruct((n, wkv.shape[1]), BF16),
        compiler_params=_cparams("parallel"),
        name="memkv",
    )(mem2, g, b, wkv)


def _xattn_kernel(alpha, heads, h_ref, kv_ref, wq_ref, wo_ref, g_ref, b_ref, o_ref):
    h = h_ref[...]
    d = h.shape[1]
    hd = d // heads
    q = _mm(h, wq_ref[...])
    kv = kv_ref[...]
    scale = np.float32(hd ** -0.5)
    hs = range(heads)
    qb = q.astype(BF16)
    s = [_mm_nt(qb[:, i * hd:(i + 1) * hd], kv[:, i * hd:(i + 1) * hd]) * scale for i in hs]
    s = [x - jnp.max(x, axis=-1, keepdims=True) for x in s]
    e = [jnp.exp(x) for x in s]
    prob = [x / jnp.sum(x, axis=-1, keepdims=True) for x in e]
    o = [_mm(prob[i], kv[:, d + i * hd:d + (i + 1) * hd]) for i in hs]
    out = _mm(jnp.concatenate(o, axis=1), wo_ref[...])
    o_ref[...] = _ln(alpha * h + out, g_ref[...], b_ref[...])


def _xattn(h1, kv, wq, wo, g, b, alpha, heads, seq, mem_len, tm):
    n, d = h1.shape
    tps = seq // tm
    return pl.pallas_call(
        functools.partial(_xattn_kernel, alpha, heads),
        grid=(n // tm,),
        in_specs=[pl.BlockSpec((tm, d), lambda i: (i, 0)),
                  pl.BlockSpec((mem_len, 2 * d), lambda i: (i // tps, 0)),
                  _full(wq.shape), _full(wo.shape), _full((1, d)), _full((1, d))],
        out_specs=pl.BlockSpec((tm, d), lambda i: (i, 0)),
        out_shape=jax.ShapeDtypeStruct((n, d), F32),
        compiler_params=_cparams("parallel"),
        name="xattn",
    )(h1, kv, wq, wo, g, b)


def _extract_topk(s, pos, k):
    rank = jnp.full(s.shape, float(k), F32)
    vals = []
    for j in range(k):
        m = jnp.max(s, axis=0, keepdims=True)
        first = jnp.min(jnp.where(s == m, pos, np.float32(np.inf)), axis=0, keepdims=True)
        sel = pos == first
        rank = jnp.where(sel, float(j), rank)
        s = jnp.where(sel, -jnp.inf, s)
        vals.append(m)
    return vals, rank


def _bf16_pair_word(x):
    hi = pltpu.bitcast(x.astype(BF16).astype(F32), jnp.int32) & jnp.int32(-65536)
    return hi | lax.shift_right_logical(hi, 16)


def _candidate_layout(k):
    pieces = [("row", 0, 0, 16), ("row", 1, 0, 8), ("col", 0, 0, 16), ("col", 1, 0, 8),
              ("col", 2, 0, 8), ("col", 3, 0, 8), ("col", 4, 0, 8)]
    pos, seen = [], set()
    for kind, fixed, start, n in pieces:
        for o in range(start, start + n):
            i, j = (fixed, o) if kind == "row" else (o, fixed)
            ok = (i + 1) * (j + 1) <= k and (i, j) not in seen
            if ok:
                seen.add((i, j))
            pos.append(float(i * k + j) if ok else -1.0)
    assert len(seen) == sum(k // (i + 1) for i in range(k))
    return pieces, np.asarray(pos, np.float32)


def _sort16_comparators():
    def merge(lo, hi, r):
        step = r * 2
        if step < hi - lo:
            yield from merge(lo, hi, step)
            yield from merge(lo + r, hi, step)
            yield from ((i, i + r) for i in range(lo + r, hi - r, step))
        else:
            yield (lo, lo + r)

    def sort(lo, hi):
        if hi - lo >= 1:
            mid = lo + (hi - lo) // 2
            yield from sort(lo, mid)
            yield from sort(mid + 1, hi)
            yield from merge(lo, hi, 1)

    return tuple(sort(0, 15))


def _top16_sorted(s):
    def exchange(v, i, j):
        v[i], v[j] = jnp.maximum(v[i], v[j]), jnp.minimum(v[i], v[j])

    v = [s[8 * r:8 * r + 8] for r in range(16)]
    for i, j in _sort16_comparators():
        exchange(v, i, j)
    for shift in (4, 2, 1):
        w = [pltpu.roll(x, shift, 0) for x in v]
        v = [jnp.maximum(v[i], w[15 - i]) for i in range(16)]
        for stride in (8, 4, 2, 1):
            for i in range(16):
                if not i & stride:
                    exchange(v, i, i + stride)
    return v


def _route_kernel(heads, h_ref, wq_ref, sk_ref, cpos_ref, cnt_ref, e1_ref, rank2_ref, e2_ref, q_scr):
    h = h_ref[...]
    tt = h.shape[0]
    q_scr[...] = _mm(h, wq_ref[...])
    k = TOPK
    pieces, _ = _candidate_layout(k)
    cpos = cpos_ref[...]
    cvalid = cpos >= 0.0
    cpos_inf = jnp.where(cvalid, cpos, np.float32(np.inf))
    key_pos = lax.broadcasted_iota(jnp.int32, (N_KEYS, tt), 0).astype(F32)
    row16 = lax.broadcasted_iota(jnp.int32, (k, 1), 0)

    def scores(hd):
        off = pl.multiple_of(hd * (2 * LANES), 2 * LANES)
        s1 = _mm_nt_hi(sk_ref[0], q_scr[:, pl.ds(off, LANES)])
        s2 = _mm_nt_hi(sk_ref[1], q_scr[:, pl.ds(off + LANES, LANES)])
        return s1, s2

    def pair_selection(top1, top2):
        t1 = jnp.concatenate(top1, axis=0)
        t2 = jnp.concatenate(top2, axis=0)
        parts = []
        for kind, fixed, start, n in pieces:
            if kind == "row":
                parts.append(top1[fixed] + t2[start:start + n])
            else:
                parts.append(t1[start:start + n] + top2[fixed])
        cand = jnp.where(cvalid, jnp.concatenate(parts, axis=0), -jnp.inf)
        best, crank = _extract_topk(cand, cpos_inf, k)
        chosen = jnp.where(crank < float(k), 1.0, 0.0)
        n16 = jnp.zeros((k, tt), F32)
        r0 = 0
        for kind, fixed, start, n in pieces:
            c = chosen[r0:r0 + n]
            r0 += n
            if kind == "row":
                n16 = n16 + jnp.where(row16 == fixed, jnp.sum(c, axis=0, keepdims=True), 0.0)
            elif n == k:
                n16 = n16 + c
            else:
                n16 = n16 + jnp.concatenate([c, jnp.zeros((k - n, tt), F32)], axis=0)
        return best, n16

    def emit(hd, s1, s2, max1, max2, best, cnt_dense, rank2):
        z = jnp.zeros_like(best[0])
        for j in range(k):
            z = z + jnp.exp(best[j] - best[0])
        cnt_ref[hd] = _bf16_pair_word(cnt_dense)
        e1_ref[hd] = jnp.exp(s1 - max1) / z
        rank2_ref[hd] = rank2.astype(BF16)
        e2_ref[hd] = jnp.exp(s2 - max2).astype(BF16)

    def head_by_sorting(hd, bad):
        s1, s2 = scores(hd)
        v1 = _top16_sorted(s1)
        v2 = _top16_sorted(s2)
        top1 = [x[0:1] for x in v1]
        top2 = [x[0:1] for x in v2]
        best, n16 = pair_selection(top1, top2)
        rank2 = jnp.zeros_like(s2)
        cnt_dense = jnp.zeros_like(s1)
        for i in range(k):
            rank2 = rank2 + jnp.where(top2[i] > s2, 1.0, 0.0)
            cnt_dense = cnt_dense + jnp.where(s1 == top1[i], n16[i:i + 1], 0.0)
        in1 = jnp.sum(jnp.where(s1 >= top1[k - 1], 1.0, 0.0), axis=0, keepdims=True)
        in2 = jnp.sum(jnp.where(rank2 < float(k), 1.0, 0.0), axis=0, keepdims=True)
        ties = jnp.where(in1 == float(k), 0.0, 1.0) + jnp.where(in2 == float(k), 0.0, 1.0)
        for i in range(k - 1):
            ties = ties + jnp.where(top1[i] > top1[i + 1], 0.0, 1.0) + jnp.where(top2[i] > top2[i + 1], 0.0, 1.0)
        emit(hd, s1, s2, top1[0], top2[0], best, cnt_dense, rank2)
        return bad + ties

    def head_exact(hd, carry):
        s1, s2 = scores(hd)
        top1, rank1 = _extract_topk(s1, key_pos, k)
        top2, rank2 = _extract_topk(s2, key_pos, k)
        best, n16 = pair_selection(top1, top2)
        cnt_dense = jnp.zeros_like(s1)
        for i in range(k):
            cnt_dense = cnt_dense + jnp.where(rank1 == float(i), n16[i:i + 1], 0.0)
        emit(hd, s1, s2, top1[0], top2[0], best, cnt_dense, rank2)
        return carry

    bad = lax.fori_loop(0, heads, head_by_sorting, jnp.zeros((1, tt), F32))

    @pl.when(jnp.sum(bad) > 0.0)
    def _():
        lax.fori_loop(0, heads, head_exact, 0)


def _route(h2, wq, sk, heads, tt):
    n, d = h2.shape
    _, pos = _candidate_layout(TOPK)
    cpos = jnp.asarray(np.broadcast_to(pos[:, None], (pos.shape[0], tt)).copy())
    blk = pl.BlockSpec((heads, N_KEYS, tt), lambda i: (0, 0, i))
    return pl.pallas_call(
        functools.partial(_route_kernel, heads),
        grid=(n // tt,),
        in_specs=[pl.BlockSpec((tt, d), lambda i: (i, 0)), _full(wq.shape), _full(sk.shape), _full(cpos.shape)],
        out_specs=[blk] * 4,
        out_shape=[jax.ShapeDtypeStruct((heads, N_KEYS, n), jnp.int32), jax.ShapeDtypeStruct((heads, N_KEYS, n), F32)]
                  + [jax.ShapeDtypeStruct((heads, N_KEYS, n), BF16)] * 2,
        scratch_shapes=[pltpu.VMEM((tt, wq.shape[1]), F32)],
        compiler_params=_cparams("parallel"),
        name="route",
    )(h2, wq.astype(BF16), sk, cpos)


PEER_CHUNK = 256


def _peer_kernel(alpha, heads, h_ref, u_ref, vt_ref, cnt_ref, e1_ref, rank2_ref, e2_ref, g_ref, b_ref, o_ref,
                 acc_scr, hbt_scr):
    e = pl.program_id(1)
    te = u_ref.shape[0]
    tt = h_ref.shape[0]
    sub = 16
    grp = N_KEYS // sub
    per_chunk = PEER_CHUNK // N_KEYS
    dot = functools.partial(jnp.dot, preferred_element_type=F32)

    @pl.when(e == 0)
    def _():
        acc_scr[...] = jnp.zeros_like(acc_scr)
        hbt_scr[...] = jnp.transpose(h_ref[...]).astype(BF16)

    for c in range(te // PEER_CHUNK):
        rows = slice(c * PEER_CHUNK, (c + 1) * PEER_CHUNK)
        act = dot(u_ref[rows, :], hbt_scr[...])
        ws = []
        for bl in range(per_chunk):
            ig = (e * (te // PEER_CHUNK) + c) * per_chunk + bl
            gate = jnp.zeros((grp, sub, tt), BF16)
            for hd in range(heads):
                cnt = pltpu.bitcast(jnp.broadcast_to(cnt_ref[hd, pl.ds(ig, 1), :], (sub // 2, tt)), BF16)
                e1 = jnp.broadcast_to(e1_ref[hd, pl.ds(ig, 1), :], (sub, tt)).astype(BF16)
                rank2 = rank2_ref[hd].reshape(grp, sub, tt)
                e2 = e2_ref[hd].reshape(grp, sub, tt)
                gate = gate + jnp.where(rank2 < cnt[None], e2, jnp.zeros_like(e2)) * e1[None]
            a = act[bl * N_KEYS:(bl + 1) * N_KEYS]
            ws.append((gate * _gelu(a).astype(BF16).reshape(grp, sub, tt)).reshape(N_KEYS, tt))
        acc_scr[...] += dot(vt_ref[:, rows], jnp.concatenate(ws, axis=0))

    @pl.when(e == pl.num_programs(1) - 1)
    def _():
        y = jnp.transpose(acc_scr[...])
        o_ref[...] = _ln(alpha * h_ref[...] + y, g_ref[...], b_ref[...])


def _peer(h2, u_bf, vt_bf, cnt, e1, rank2, e2, g, b, alpha, heads, tt, te):
    n, d = h2.shape
    assert te % PEER_CHUNK == 0
    blk = pl.BlockSpec((heads, N_KEYS, tt), lambda i, e: (0, 0, i))
    return pl.pallas_call(
        functools.partial(_peer_kernel, alpha, heads),
        grid=(n // tt, u_bf.shape[0] // te),
        in_specs=[pl.BlockSpec((tt, d), lambda i, e: (i, 0)),
                  pl.BlockSpec((te, d), lambda i, e: (e, 0)),
                  pl.BlockSpec((d, te), lambda i, e: (0, e)),
                  blk, blk, blk, blk, _full((1, d)), _full((1, d))],
        out_specs=pl.BlockSpec((tt, d), lambda i, e: (i, 0)),
        out_shape=jax.ShapeDtypeStruct((n, d), F32),
        scratch_shapes=[pltpu.VMEM((d, tt), F32), pltpu.VMEM((d, tt), BF16)],
        compiler_params=_cparams("parallel", "arbitrary"),
        name="peer",
    )(h2, u_bf, vt_bf, cnt, e1, rank2, e2, g, b)


def _block_diag2(w):
    z = jnp.zeros_like(w[0])
    return jnp.concatenate([jnp.concatenate([w[0], z], axis=1), jnp.concatenate([z, w[1]], axis=1)], axis=0)


def _layer(h_in, mem2, batch, seq, mem_len, lng, lnb, first, p):
    del first
    n, d = h_in.shape
    (w_in, rwkv_mu, rwkv_w0, rwkv_w2, rwkv_a0, rwkv_a2, rwkv_g2, rwkv_k_k, rwkv_k_a, rwkv_r_k,
     rwkv_gn_g, rwkv_gn_b, gmlp_ln_g, gmlp_ln_b, gmlp_w_s, gmlp_b_s, w_branch, w_mix_out, ln1_g, ln1_b,
     mem_ln_g, mem_ln_b, xattn_w_q, xattn_w_kv, xattn_w_o, ln2_g, ln2_b,
     peer_w_query, peer_sub_keys, peer_u, peer_v, ln3_g, ln3_b, alpha) = p
    width = rwkv_k_k.shape[0]
    rw_cols = rwkv_mu.shape[0]
    gm_cols = 2 * gmlp_ln_g.shape[0]
    chunk = gmlp_w_s.shape[1]
    row1 = lambda t: t.reshape(1, -1)

    tm = 256
    wa = w_in[:, :rw_cols].astype(BF16)
    wb = w_in[:, rw_cols:rw_cols + gm_cols].astype(BF16)
    wg = w_in[:, rw_cols + gm_cols:].astype(BF16)
    pa, pb, pg = _inproj(h_in, row1(lng), row1(lnb), wa, wb, wg, tm)

    heads_r = width // HEAD_DIM
    bd = jnp.asarray(np.kron(np.eye(heads_r, dtype=np.float32), np.ones((HEAD_DIM, HEAD_DIM), np.float32)), BF16)
    r, v, na, lw, kd, bv, g, bonus = _rwkv_pre(
        pa, seq, row1(rwkv_mu), row1(rwkv_w0), _hilo(_block_diag2(rwkv_w2)), row1(rwkv_a0),
        _hilo(_block_diag2(rwkv_a2)), _hilo(rwkv_g2), row1(rwkv_k_k), row1(rwkv_k_a), row1(rwkv_r_k), bd, tm)
    yf, yb = _scan(r, v, na, lw, kd, bv, batch, seq)

    groups = gmlp_w_s.shape[0]
    wscat = jnp.concatenate([gmlp_w_s[0::2], gmlp_w_s[1::2]], axis=2).astype(BF16)
    bsfull = jnp.repeat(gmlp_b_s.T, HEAD_DIM, axis=1)
    assert groups * HEAD_DIM == width
    h1 = _mixer(h_in, yf, yb, g, bonus, pb, pg, row1(lng), row1(lnb), row1(rwkv_gn_g), row1(rwkv_gn_b), bd,
                row1(gmlp_ln_g), row1(gmlp_ln_b), wscat, bsfull, w_branch.astype(BF16), w_mix_out.astype(BF16),
                row1(ln1_g), row1(ln1_b), alpha, chunk, tm)

    kv = _memkv(mem2, row1(mem_ln_g), row1(mem_ln_b), xattn_w_kv.astype(BF16), mem_len)
    xheads = 4
    h2 = _xattn(h1, kv, xattn_w_q.astype(BF16), xattn_w_o.astype(BF16), row1(ln2_g), row1(ln2_b),
                alpha, xheads, seq, mem_len, tm)

    pheads = peer_w_query.shape[1] // (2 * LANES)
    cnt, e1, rank2, e2 = _route(h2, peer_w_query, peer_sub_keys, pheads, 256)
    h3 = _peer(h2, peer_u.astype(BF16), peer_v.T.astype(BF16), cnt, e1, rank2, e2, row1(ln3_g), row1(ln3_b),
               alpha, pheads, 256, 2048)
    return h3


def kernel(x, mem, ln_emb_g, ln_emb_b, w_in, rwkv_mu, rwkv_w0, rwkv_w2, rwkv_a0, rwkv_a2, rwkv_g2, rwkv_k_k, rwkv_k_a, rwkv_r_k, rwkv_gn_g, rwkv_gn_b, gmlp_ln_g, gmlp_ln_b, gmlp_w_s, gmlp_b_s, w_branch, w_mix_out, ln1_g, ln1_b, mem_ln_g, mem_ln_b, xattn_w_q, xattn_w_kv, xattn_w_o, ln2_g, ln2_b, peer_w_query, peer_sub_keys, peer_u, peer_v, ln3_g, ln3_b):
    batch, seq, d = x.shape
    mem_len = mem.shape[1]
    depth = w_in.shape[0]
    assert depth == 1, "the layer pipeline applies the embedding LN inside the first layer only"
    alpha = float((2.0 * depth) ** 0.25)
    x2 = x.reshape(batch * seq, d)
    mem2 = mem.reshape(batch * mem_len, d)
    flat = lambda t: t.reshape(-1)
    p = (w_in[0], rwkv_mu[0], flat(rwkv_w0[0]), rwkv_w2[0], flat(rwkv_a0[0]), rwkv_a2[0], rwkv_g2[0],
         rwkv_k_k[0], rwkv_k_a[0], flat(rwkv_r_k[0]), rwkv_gn_g[0], rwkv_gn_b[0], gmlp_ln_g[0], gmlp_ln_b[0],
         gmlp_w_s[0], gmlp_b_s[0], w_branch[0], w_mix_out[0], ln1_g[0], ln1_b[0], mem_ln_g[0], mem_ln_b[0],
         xattn_w_q[0], xattn_w_kv[0], xattn_w_o[0], ln2_g[0], ln2_b[0], peer_w_query[0], peer_sub_keys[0],
         peer_u[0], peer_v[0], ln3_g[0], ln3_b[0], alpha)
    h = _layer(x2, mem2, batch, seq, mem_len, ln_emb_g, ln_emb_b, True, p)
    return h.reshape(batch, seq, d)
```

```python
import functools

import numpy as np
import jax
import jax.numpy as jnp
from jax import lax
from jax.experimental import pallas as pl
from jax.experimental.pallas import tpu as pltpu

F32 = jnp.float32
BF16 = jnp.bfloat16
HIGHEST = lax.Precision.HIGHEST

LN_EPS = 1e-5
GN_EPS = 64e-5
LANES = 128
HEAD_DIM = 64
SCAN_CHUNK = 64
TOPK = 16
N_KEYS = 128
XATTN_HEADS = 4
TOKEN_TILE = 256
PEER_EXPERT_TILE = 2048
VMEM_LIMIT = 56 * 1024 * 1024


def _cparams(*sem):
    return pltpu.CompilerParams(dimension_semantics=sem, vmem_limit_bytes=VMEM_LIMIT)


def _ln(x, g, b, eps=LN_EPS):
    mu = jnp.mean(x, axis=-1, keepdims=True)
    xc = x - mu
    var = jnp.mean(xc * xc, axis=-1, keepdims=True)
    return xc * lax.rsqrt(var + eps) * g + b


def _mm(a, b):
    return jnp.dot(a.astype(BF16), b.astype(BF16), preferred_element_type=F32)


def _mm_nt(a, b):
    return lax.dot_general(a, b, (((1,), (1,)), ((), ())), preferred_element_type=F32)


def _mm_nt_hi(a, b):
    return lax.dot_general(a, b, (((1,), (1,)), ((), ())), precision=HIGHEST,
                           preferred_element_type=F32)


def _split_bf16(x):
    hi = x.astype(BF16)
    return hi, (x - hi.astype(F32)).astype(BF16)


def _hilo(w):
    hi, lo = _split_bf16(w)
    return jnp.stack([hi, lo])


def _mm_x3(x, w_ref):
    hi, lo = _split_bf16(x)
    w_hi = w_ref[0]
    return (jnp.dot(hi, w_hi, preferred_element_type=F32) + jnp.dot(hi, w_ref[1], preferred_element_type=F32)
            + jnp.dot(lo, w_hi, preferred_element_type=F32))


def _segsum(x, ones):
    hi, lo = _split_bf16(x)
    return jnp.dot(hi, ones, preferred_element_type=F32) + jnp.dot(lo, ones, preferred_element_type=F32)


def _gelu_x2(x):
    return x * (1.0 + lax.erf(x * np.float32(1.0 / np.sqrt(2.0))))


def _gelu(x):
    return 0.5 * _gelu_x2(x)


def _sigmoid(x):
    return 1.0 / (1.0 + jnp.exp(-x))


def _full(shape):
    nd = len(shape)
    return pl.BlockSpec(shape, lambda *_: (0,) * nd)


def _inproj_kernel(x_ref, g_ref, b_ref, wa_ref, wb_ref, wg_ref, pa_ref, pb_ref, pg_ref):
    h = _ln(x_ref[...], g_ref[...], b_ref[...]).astype(BF16)
    pa_ref[...] = jnp.dot(h, wa_ref[...], preferred_element_type=F32)
    pb_ref[...] = _gelu(jnp.dot(h, wb_ref[...], preferred_element_type=F32))
    pg_ref[...] = _sigmoid(jnp.dot(h, wg_ref[...], preferred_element_type=F32))


def _inproj(x2, g, b, wa, wb, wg, tm):
    n, d = x2.shape
    ca, cb, cg = wa.shape[1], wb.shape[1], wg.shape[1]
    row = lambda c: pl.BlockSpec((tm, c), lambda i: (i, 0))
    return pl.pallas_call(
        _inproj_kernel,
        grid=(n // tm,),
        in_specs=[row(d), _full((1, d)), _full((1, d)), _full((d, ca)), _full((d, cb)), _full((d, cg))],
        out_specs=[row(ca), row(cb), row(cg)],
        out_shape=[jax.ShapeDtypeStruct((n, c), F32) for c in (ca, cb, cg)],
        compiler_params=_cparams("parallel"),
        name="inproj",
    )(x2, g, b, wa, wb, wg)


def _rwkv_pre_kernel(tiles_per_seq, width,
                     p_ref, prev_ref, next_ref, mu_ref, w0_ref, w2_ref, a0_ref, a2_ref, g2_ref,
                     kk_ref, ka_ref, rk_ref, bd_ref,
                     r_out, v_out, na_out, lw_out, kd_out, bv_out, g_out, bonus_out):
    i = pl.program_id(0)
    pos = i % tiles_per_seq
    p = p_ref[...]
    tm = p.shape[0]
    row = lax.broadcasted_iota(jnp.int32, (tm, 1), 0)
    prev_row = jnp.where(pos == 0, 0.0, prev_ref[7:8, :])
    next_row = jnp.where(pos == tiles_per_seq - 1, 0.0, next_ref[0:1, :])
    prev = jnp.where(row == 0, prev_row, pltpu.roll(p, 1, 0))
    nxt = jnp.where(row == tm - 1, next_row, pltpu.roll(p, tm - 1, 0))
    ps = p + mu_ref[...] * (0.5 * (prev + nxt) - p)

    w = width
    r = ps[:, 0:w]
    k = ps[:, w:2 * w]
    v = ps[:, 2 * w:3 * w]
    wd = ps[:, 3 * w:3 * w + LANES]
    ad = ps[:, 3 * w + LANES:3 * w + 2 * LANES]
    gd = ps[:, 3 * w + 2 * LANES:3 * w + 3 * LANES]

    wz = w0_ref[...] + _mm_x3(jnp.tanh(wd), w2_ref)
    log_decay = np.float32(-np.exp(-0.5)) * _sigmoid(wz)
    a = _sigmoid(a0_ref[...] + _mm_x3(ad, a2_ref))
    g = _mm_x3(_sigmoid(gd), g2_ref)

    bd = bd_ref[...]
    kk = k * kk_ref[...]
    nrm = jnp.sqrt(_segsum(kk * kk, bd))
    kk = kk / jnp.maximum(nrm, 1e-12)

    ka = ka_ref[...]
    k0 = k * (1.0 + (a[:, 0:w] - 1.0) * ka)
    k1 = k * (1.0 + (a[:, w:2 * w] - 1.0) * ka)
    rk = _segsum(r * (k0 + k1) * rk_ref[...], bd)

    r_out[...] = r
    v_out[...] = v
    na_out[...] = -kk
    lw_out[0] = log_decay[:, 0:w]
    lw_out[1] = log_decay[:, w:2 * w]
    kd_out[0] = k0
    kd_out[1] = k1
    bv_out[0] = kk * a[:, 0:w]
    bv_out[1] = kk * a[:, w:2 * w]
    g_out[...] = g
    bonus_out[...] = rk * v


def _rwkv_pre(pa, seq, mu, w0, w2bd, a0, a2bd, g2, k_k, k_a, r_k, bd, tm):
    n, ca = pa.shape
    w = bd.shape[0]
    tps = seq // tm
    hb = tm // 8
    nb8 = n // 8
    row = lambda c: pl.BlockSpec((tm, c), lambda i: (i, 0))
    row2 = pl.BlockSpec((2, tm, w), lambda i: (0, i, 0))
    outs = [jax.ShapeDtypeStruct((n, w), F32)] * 3 + [jax.ShapeDtypeStruct((2, n, w), F32)] * 3 + \
           [jax.ShapeDtypeStruct((n, w), F32)] * 2
    return pl.pallas_call(
        functools.partial(_rwkv_pre_kernel, tps, w),
        grid=(n // tm,),
        in_specs=[row(ca),
                  pl.BlockSpec((8, ca), lambda i: (jnp.maximum(i * hb - 1, 0), 0)),
                  pl.BlockSpec((8, ca), lambda i: (jnp.minimum((i + 1) * hb, nb8 - 1), 0)),
                  _full((1, ca)), _full((1, 2 * w)), _full(w2bd.shape), _full((1, 2 * w)),
                  _full(a2bd.shape), _full(g2.shape), _full((1, w)), _full((1, w)), _full((1, w)),
                  _full((w, w))],
        out_specs=[row(w), row(w), row(w), row2, row2, row2, row(w), row(w)],
        out_shape=outs,
        compiler_params=_cparams("parallel"),
        name="rwkv_pre",
    )(pa, pa, pa, mu, w0, w2bd, a0, a2bd, g2, k_k, k_a, r_k, bd)


def _scan_kernel(rf_ref, vf_ref, af_ref, rb_ref, vb_ref, ab_ref, lwf_ref, kf_ref, bf_ref, lwb_ref, kb_ref, bb_ref,
                 ms_ref, mi_ref, lc_ref, yf_ref, yb_ref, s_scr):
    c = SCAN_CHUNK
    h2 = 2 * c

    @pl.when(pl.program_id(1) == 0)
    def _():
        s_scr[...] = jnp.zeros_like(s_scr)

    first_head = lax.broadcasted_iota(jnp.int32, (1, LANES), 1) < HEAD_DIM

    def stack(x):
        return jnp.concatenate([jnp.where(first_head, x, 0.0), jnp.where(first_head, 0.0, x)], axis=0)

    dot = functools.partial(jnp.dot, preferred_element_type=F32)
    n_groups = lwf_ref.shape[1] // LANES
    sls = [slice(m * LANES, (m + 1) * LANES) for m in range(n_groups)]

    ar, bk, v_s, bk_e, p_end, strict, incl = [], [], [], [], [], [], []
    dirs = ((rf_ref, vf_ref, af_ref, lwf_ref, kf_ref, bf_ref), (rb_ref, vb_ref, ab_ref, lwb_ref, kb_ref, bb_ref))
    for d, (r_ref, v_ref, a_ref, lw_ref, k_ref, b_ref) in enumerate(dirs):
        lw = lw_ref[...]
        l1 = lw.astype(BF16)
        rest = lw - l1.astype(F32)
        l2 = rest.astype(BF16)
        l3 = (rest - l2.astype(F32)).astype(BF16)
        lc = lc_ref[d]
        cum = dot(lc, l1) + dot(lc, l2) + dot(lc, l3)
        tot = jnp.sum(lw, axis=0, keepdims=True)
        e_neg = jnp.exp(-cum)
        e_end = jnp.exp(tot - cum)
        a_t = a_ref[...] * jnp.exp(cum - lw)
        r_t = r_ref[...] * jnp.exp(cum)
        b_in = b_ref[...]
        k_in = k_ref[...]
        b_t = b_in * e_neg
        k_t = k_in * e_neg
        b_e = b_in * e_end
        k_e = k_in * e_end
        v_in = v_ref[...]
        pe = jnp.exp(tot)
        sd = ms_ref[d] > 0.0
        ic = mi_ref[d] > 0.0
        for s in sls:
            ar.append(jnp.concatenate([stack(a_t[:, s]), stack(r_t[:, s])], axis=0).astype(BF16))
            bk.append(jnp.concatenate([stack(b_t[:, s]), stack(k_t[:, s])], axis=0).astype(BF16))
            bk_e.append(jnp.concatenate([stack(b_e[:, s]), stack(k_e[:, s])], axis=0).astype(BF16))
            v_s.append(stack(v_in[:, s]))
            p_end.append(pe[:, s])
            strict.append(sd)
            incl.append(ic)
    gs = range(2 * n_groups)
    v_b = [x.astype(BF16) for x in v_s]
    state = [s_scr[m] for m in gs]

    l_all = [_mm_nt(ar[m], bk[m]) for m in gs]
    from_state = [_mm_nt(ar[m], state[m].astype(BF16)) for m in gs]
    l_ab = [jnp.where(strict[m], l_all[m][:h2, :h2], 0.0).astype(BF16) for m in gs]
    l_k = [jnp.concatenate([jnp.where(strict[m], l_all[m][:h2, h2:], 0.0),
                            jnp.where(incl[m], l_all[m][h2:, h2:], 0.0)], axis=0).astype(BF16) for m in gs]
    l_rb = [jnp.where(incl[m], l_all[m][h2:, :h2], 0.0).astype(BF16) for m in gs]
    from_v = [dot(l_k[m], v_b[m]) for m in gs]

    x = [from_state[m][:h2] + from_v[m][:h2] for m in gs]
    lp = l_ab
    x = [x[m] + dot(lp[m], x[m].astype(BF16)) for m in gs]
    span = 2
    while span < c:
        lp = [dot(lp[m], lp[m]).astype(BF16) for m in gs]
        x = [x[m] + dot(lp[m], x[m].astype(BF16)) for m in gs]
        span *= 2
    u_s = x

    y_s = [from_state[m][h2:] + from_v[m][h2:] + dot(l_rb[m], u_s[m].astype(BF16)) for m in gs]
    uv_t = [jnp.transpose(jnp.concatenate([u_s[m], v_s[m]], axis=0)).astype(BF16) for m in gs]
    new_state = [state[m] * p_end[m] + dot(uv_t[m], bk_e[m]) for m in gs]
    for m in gs:
        y_ref = yf_ref if m < n_groups else yb_ref
        y_ref[:, sls[m % n_groups]] = y_s[m][:c] + y_s[m][c:]
        s_scr[m] = new_state[m]


def _scan(r, v, na, lw, kd, bv, batch, seq):
    n, w = r.shape
    c = SCAN_CHUNK
    nc = seq // c
    t = np.arange(c)
    before = [(t[None, :] < t[:, None]), (t[None, :] > t[:, None])]
    eye = np.eye(c, dtype=bool)
    tile2 = lambda m: np.tile(m, (2, 2))
    ms = jnp.asarray(np.stack([tile2(m) for m in before]).astype(np.float32))
    mi = jnp.asarray(np.stack([tile2(m | eye) for m in before]).astype(np.float32))
    lc = jnp.asarray(np.stack([(m | eye) for m in before]).astype(np.float32), BF16)

    fwd = pl.BlockSpec((c, w), lambda b, ci: (b * nc + ci, 0))
    bwd = pl.BlockSpec((c, w), lambda b, ci: (b * nc + nc - 1 - ci, 0))
    fwd_d = pl.BlockSpec((None, c, w), lambda b, ci: (0, b * nc + ci, 0))
    bwd_d = pl.BlockSpec((None, c, w), lambda b, ci: (1, b * nc + nc - 1 - ci, 0))
    return pl.pallas_call(
        _scan_kernel,
        grid=(batch, nc),
        in_specs=[fwd, fwd, fwd, bwd, bwd, bwd, fwd_d, fwd_d, fwd_d, bwd_d, bwd_d, bwd_d,
                  _full(ms.shape), _full(mi.shape), _full(lc.shape)],
        out_specs=[fwd, bwd],
        out_shape=[jax.ShapeDtypeStruct((n, w), F32)] * 2,
        scratch_shapes=[pltpu.VMEM((2 * (w // LANES), LANES, LANES), F32)],
        compiler_params=_cparams("parallel", "arbitrary"),
        name="scan",
    )(r, v, na, r, v, na, lw, kd, bv, lw, kd, bv, ms, mi, lc)


def _mixer_kernel(alpha, chunk,
                  x_ref, yf_ref, yb_ref, g_ref, bonus_ref, pb_ref, pg_ref,
                  lng_ref, lnb_ref, gng_ref, gnb_ref, bd_ref, mlg_ref, mlb_ref, ws_ref, bs_ref,
                  wbr_ref, wmix_ref, l1g_ref, l1b_ref, o_ref):
    h0 = _ln(x_ref[...], lng_ref[...], lnb_ref[...])
    bd = bd_ref[...]
    inv = np.float32(1.0 / HEAD_DIM)

    y = yf_ref[...] + yb_ref[...]
    mu = _segsum(y, bd) * inv
    yc = y - mu
    var = _segsum(yc * yc, bd) * inv
    y_a = (yc * lax.rsqrt(var + GN_EPS) * gng_ref[...] + gnb_ref[...] + bonus_ref[...]) * g_ref[...]

    gp = pb_ref[...]
    w = gp.shape[1] // 2
    u = gp[:, :w]
    vn = _ln(gp[:, w:], mlg_ref[...], mlb_ref[...])
    tm = u.shape[0]
    first_group = lax.broadcasted_iota(jnp.int32, (1, LANES), 1) < HEAD_DIM
    rows = []
    for ci in range(tm // chunk):
        vc = vn[ci * chunk:(ci + 1) * chunk]
        cols = []
        for m in range(w // LANES):
            vp = vc[:, m * LANES:(m + 1) * LANES]
            stacked = jnp.concatenate([jnp.where(first_group, vp, 0.0), jnp.where(first_group, 0.0, vp)], axis=0)
            cols.append(_mm(ws_ref[m], stacked))
        rows.append(jnp.concatenate(cols, axis=1) + bs_ref[...])
    sv = jnp.concatenate(rows, axis=0)
    y_b = u * sv

    pg = pg_ref[...]
    d = pg.shape[1] // 2
    merged = pg[:, :d] * _mm(y_a, wbr_ref[0]) + pg[:, d:] * _mm(y_b, wbr_ref[1])
    mix = _mm(merged, wmix_ref[...])
    o_ref[...] = _ln(alpha * h0 + mix, l1g_ref[...], l1b_ref[...])


def _mixer(x2, yf, yb, g, bonus, pb, pg, lng, lnb, gng, gnb, bd, mlg, mlb, wscat, bsfull, wbr, wmix, l1g, l1b,
           alpha, chunk, tm):
    n, d = x2.shape
    w = g.shape[1]
    row = lambda c: pl.BlockSpec((tm, c), lambda i: (i, 0))
    return pl.pallas_call(
        functools.partial(_mixer_kernel, alpha, chunk),
        grid=(n // tm,),
        in_specs=[row(d), row(w), row(w), row(w), row(w), row(2 * w), row(2 * d),
                  _full((1, d)), _full((1, d)), _full((1, w)), _full((1, w)), _full((w, w)),
                  _full((1, w)), _full((1, w)), _full(wscat.shape), _full(bsfull.shape),
                  _full(wbr.shape), _full(wmix.shape), _full((1, d)), _full((1, d))],
        out_specs=row(d),
        out_shape=jax.ShapeDtypeStruct((n, d), F32),
        compiler_params=_cparams("parallel"),
        name="mixer",
    )(x2, yf, yb, g, bonus, pb, pg, lng, lnb, gng, gnb, bd, mlg, mlb, wscat, bsfull, wbr, wmix, l1g, l1b)


def _memkv_kernel(m_ref, g_ref, b_ref, w_ref, o_ref):
    o_ref[...] = _mm(_ln(m_ref[...], g_ref[...], b_ref[...]), w_ref[...]).astype(BF16)


def _memkv(mem2, g, b, wkv, tm):
    n, d = mem2.shape
    return pl.pallas_call(
        _memkv_kernel,
        grid=(n // tm,),
        in_specs=[pl.BlockSpec((tm, d), lambda i: (i, 0)), _full((1, d)), _full((1, d)), _full(wkv.shape)],
        out_specs=pl.BlockSpec((tm, wkv.shape[1]), lambda i: (i, 0)),
        out_shape=jax.ShapeDtypeStruct((n, wkv.shape[1]), BF16),
        compiler_params=_cparams("parallel"),
        name="memkv",
    )(mem2, g, b, wkv)


def _xattn_kernel(alpha, heads, h_ref, kv_ref, wq_ref, wo_ref, g_ref, b_ref, o_ref):
    h = h_ref[...]
    d = h.shape[1]
    hd = d // heads
    q = _mm(h, wq_ref[...])
    kv = kv_ref[...]
    scale = np.float32(hd ** -0.5)
    hs = range(heads)
    qb = q.astype(BF16)
    s = [_mm_nt(qb[:, i * hd:(i + 1) * hd], kv[:, i * hd:(i + 1) * hd]) * scale for i in hs]
    s = [x - jnp.max(x, axis=-1, keepdims=True) for x in s]
    e = [jnp.exp(x) for x in s]
    prob = [x / jnp.sum(x, axis=-1, keepdims=True) for x in e]
    o = [_mm(prob[i], kv[:, d + i * hd:d + (i + 1) * hd]) for i in hs]
    out = _mm(jnp.concatenate(o, axis=1), wo_ref[...])
    o_ref[...] = _ln(alpha * h + out, g_ref[...], b_ref[...])


def _xattn(h1, kv, wq, wo, g, b, alpha, heads, seq, mem_len, tm):
    n, d = h1.shape
    tps = seq // tm
    return pl.pallas_call(
        functools.partial(_xattn_kernel, alpha, heads),
        grid=(n // tm,),
        in_specs=[pl.BlockSpec((tm, d), lambda i: (i, 0)),
                  pl.BlockSpec((mem_len, 2 * d), lambda i: (i // tps, 0)),
                  _full(wq.shape), _full(wo.shape), _full((1, d)), _full((1, d))],
        out_specs=pl.BlockSpec((tm, d), lambda i: (i, 0)),
        out_shape=jax.ShapeDtypeStruct((n, d), F32),
        compiler_params=_cparams("parallel"),
        name="xattn",
    )(h1, kv, wq, wo, g, b)


def _extract_topk(s, pos, k):
    rank = jnp.full(s.shape, float(k), F32)
    vals = []
    for j in range(k):
        m = jnp.max(s, axis=0, keepdims=True)
        first = jnp.min(jnp.where(s == m, pos, np.float32(np.inf)), axis=0, keepdims=True)
        sel = pos == first
        rank = jnp.where(sel, float(j), rank)
        s = jnp.where(sel, -jnp.inf, s)
        vals.append(m)
    return vals, rank


def _candidate_layout(k):
    pieces = [("row", 0, 0, 16), ("row", 1, 0, 8), ("col", 0, 0, 16), ("col", 1, 0, 8),
              ("col", 2, 0, 8), ("col", 3, 0, 8), ("col", 4, 0, 8)]
    pos, seen = [], set()
    for kind, fixed, start, n in pieces:
        for o in range(start, start + n):
            i, j = (fixed, o) if kind == "row" else (o, fixed)
            ok = (i + 1) * (j + 1) <= k and (i, j) not in seen
            if ok:
                seen.add((i, j))
            pos.append(float(i * k + j) if ok else -1.0)
    assert len(seen) == sum(k // (i + 1) for i in range(k))
    return pieces, np.asarray(pos, np.float32)


def _sort16_comparators():
    def merge(lo, hi, r):
        step = r * 2
        if step < hi - lo:
            yield from merge(lo, hi, step)
            yield from merge(lo + r, hi, step)
            yield from ((i, i + r) for i in range(lo + r, hi - r, step))
        else:
            yield (lo, lo + r)

    def sort(lo, hi):
        if hi - lo >= 1:
            mid = lo + (hi - lo) // 2
            yield from sort(lo, mid)
            yield from sort(mid + 1, hi)
            yield from merge(lo, hi, 1)

    return tuple(sort(0, 15))


def _top16_sorted(s):
    def exchange(v, i, j):
        v[i], v[j] = jnp.maximum(v[i], v[j]), jnp.minimum(v[i], v[j])

    v = [s[8 * r:8 * r + 8] for r in range(16)]
    for i, j in _sort16_comparators():
        exchange(v, i, j)
    for shift in (4, 2, 1):
        w = [pltpu.roll(x, shift, 0) for x in v]
        v = [jnp.maximum(v[i], w[15 - i]) for i in range(16)]
        for stride in (8, 4, 2, 1):
            for i in range(16):
                if not i & stride:
                    exchange(v, i, i + stride)
    return v


def _route_kernel(heads, h_ref, wq_ref, sk_ref, cpos_ref, cnt_ref, e1_ref, rank2_ref, e2_ref, q_scr):
    h = h_ref[...]
    tt = h.shape[0]
    q_scr[...] = _mm(h, wq_ref[...])
    k = TOPK
    pieces, _ = _candidate_layout(k)
    cpos = cpos_ref[...]
    cvalid = cpos >= 0.0
    cpos_inf = jnp.where(cvalid, cpos, np.float32(np.inf))
    key_pos = lax.broadcasted_iota(jnp.int32, (N_KEYS, tt), 0).astype(F32)
    row16 = lax.broadcasted_iota(jnp.int32, (k, 1), 0)

    def scores(hd):
        off = pl.multiple_of(hd * (2 * LANES), 2 * LANES)
        s1 = _mm_nt_hi(sk_ref[0], q_scr[:, pl.ds(off, LANES)])
        s2 = _mm_nt_hi(sk_ref[1], q_scr[:, pl.ds(off + LANES, LANES)])
        return s1, s2

    def pair_selection(top1, top2):
        t1 = jnp.concatenate(top1, axis=0)
        t2 = jnp.concatenate(top2, axis=0)
        parts = []
        for kind, fixed, start, n in pieces:
            if kind == "row":
                parts.append(top1[fixed] + t2[start:start + n])
            else:
                parts.append(t1[start:start + n] + top2[fixed])
        cand = jnp.where(cvalid, jnp.concatenate(parts, axis=0), -jnp.inf)
        best, crank = _extract_topk(cand, cpos_inf, k)
        chosen = jnp.where(crank < float(k), 1.0, 0.0)
        n16 = jnp.zeros((k, tt), F32)
        r0 = 0
        for kind, fixed, start, n in pieces:
            c = chosen[r0:r0 + n]
            r0 += n
            if kind == "row":
                n16 = n16 + jnp.where(row16 == fixed, jnp.sum(c, axis=0, keepdims=True), 0.0)
            elif n == k:
                n16 = n16 + c
            else:
                n16 = n16 + jnp.concatenate([c, jnp.zeros((k - n, tt), F32)], axis=0)
        return best, n16

    def emit(hd, s1, s2, max1, max2, best, cnt_dense, rank2):
        z = jnp.zeros_like(best[0])
        for j in range(k):
            z = z + jnp.exp(best[j] - best[0])
        cnt_ref[hd] = cnt_dense
        e1_ref[hd] = jnp.exp(s1 - max1) / z
        rank2_ref[hd] = rank2.astype(BF16)
        e2_ref[hd] = jnp.exp(s2 - max2).astype(BF16)

    def head_by_sorting(hd, bad):
        s1, s2 = scores(hd)
        v1 = _top16_sorted(s1)
        v2 = _top16_sorted(s2)
        top1 = [x[0:1] for x in v1]
        top2 = [x[0:1] for x in v2]
        best, n16 = pair_selection(top1, top2)
        rank2 = jnp.full_like(s2, float(k))
        cnt_dense = jnp.zeros_like(s1)
        for i in reversed(range(k)):
            rank2 = jnp.where(s2 >= top2[i], float(i), rank2)
            cnt_dense = jnp.where(s1 == top1[i], n16[i:i + 1], cnt_dense)
        in1 = jnp.sum(jnp.where(s1 >= top1[k - 1], 1.0, 0.0), axis=0, keepdims=True)
        in2 = jnp.sum(jnp.where(rank2 < float(k), 1.0, 0.0), axis=0, keepdims=True)
        ties = jnp.where(in1 == float(k), 0.0, 1.0) + jnp.where(in2 == float(k), 0.0, 1.0)
        for i in range(k - 1):
            ties = ties + jnp.where(top1[i] > top1[i + 1], 0.0, 1.0) + jnp.where(top2[i] > top2[i + 1], 0.0, 1.0)
        emit(hd, s1, s2, top1[0], top2[0], best, cnt_dense, rank2)
        return bad + ties

    def head_exact(hd, carry):
        s1, s2 = scores(hd)
        top1, rank1 = _extract_topk(s1, key_pos, k)
        top2, rank2 = _extract_topk(s2, key_pos, k)
        best, n16 = pair_selection(top1, top2)
        cnt_dense = jnp.zeros_like(s1)
        for i in range(k):
            cnt_dense = cnt_dense + jnp.where(rank1 == float(i), n16[i:i + 1], 0.0)
        emit(hd, s1, s2, top1[0], top2[0], best, cnt_dense, rank2)
        return carry

    bad = lax.fori_loop(0, heads, head_by_sorting, jnp.zeros((1, tt), F32))

    @pl.when(jnp.sum(bad) > 0.0)
    def _():
        lax.fori_loop(0, heads, head_exact, 0)


def _route(h2, wq, sk, heads, tt):
    n, d = h2.shape
    _, pos = _candidate_layout(TOPK)
    cpos = jnp.asarray(np.broadcast_to(pos[:, None], (pos.shape[0], tt)).copy())
    blk = pl.BlockSpec((heads, N_KEYS, tt), lambda i: (0, 0, i))
    return pl.pallas_call(
        functools.partial(_route_kernel, heads),
        grid=(n // tt,),
        in_specs=[pl.BlockSpec((tt, d), lambda i: (i, 0)), _full(wq.shape), _full(sk.shape), _full(cpos.shape)],
        out_specs=[blk] * 4,
        out_shape=[jax.ShapeDtypeStruct((heads, N_KEYS, n), F32)] * 2
                  + [jax.ShapeDtypeStruct((heads, N_KEYS, n), BF16)] * 2,
        scratch_shapes=[pltpu.VMEM((tt, wq.shape[1]), F32)],
        compiler_params=_cparams("parallel"),
        name="route",
    )(h2, wq.astype(BF16), sk, cpos)


PEER_CHUNK = 256


def _peer_kernel(alpha, heads, h_ref, u_ref, vt_ref, cnt_ref, e1_ref, rank2_ref, e2_ref, g_ref, b_ref, o_ref,
                 acc_scr, hbt_scr):
    e = pl.program_id(1)
    te = u_ref.shape[0]
    tt = h_ref.shape[0]
    sub = 16
    grp = N_KEYS // sub
    per_chunk = PEER_CHUNK // N_KEYS
    dot = functools.partial(jnp.dot, preferred_element_type=F32)

    @pl.when(e == 0)
    def _():
        acc_scr[...] = jnp.zeros_like(acc_scr)
        hbt_scr[...] = jnp.transpose(h_ref[...]).astype(BF16)

    for c in range(te // PEER_CHUNK):
        rows = slice(c * PEER_CHUNK, (c + 1) * PEER_CHUNK)
        act = dot(u_ref[rows, :], hbt_scr[...])
        ws = []
        for bl in range(per_chunk):
            il = c * per_chunk + bl
            r = il % 8
            base = pl.multiple_of(e * (te // N_KEYS) + (il - r), 8)
            gate = jnp.zeros((grp, sub, tt), BF16)
            for hd in range(heads):
                cnt = jnp.broadcast_to(cnt_ref[hd, pl.ds(base, 8), :][r:r + 1], (sub, tt)).astype(BF16)
                e1 = jnp.broadcast_to(e1_ref[hd, pl.ds(base, 8), :][r:r + 1], (sub, tt)).astype(BF16)
                rank2 = rank2_ref[hd].reshape(grp, sub, tt)
                e2 = e2_ref[hd].reshape(grp, sub, tt)
                gate = gate + jnp.where(rank2 < cnt[None], e2, jnp.zeros_like(e2)) * e1[None]
            a = act[bl * N_KEYS:(bl + 1) * N_KEYS]
            ws.append((gate * _gelu_x2(a).astype(BF16).reshape(grp, sub, tt)).reshape(N_KEYS, tt))
        acc_scr[...] += dot(vt_ref[:, rows], jnp.concatenate(ws, axis=0))

    @pl.when(e == pl.num_programs(1) - 1)
    def _():
        y = jnp.transpose(acc_scr[...])
        o_ref[...] = _ln(alpha * h_ref[...] + y, g_ref[...], b_ref[...])


def _peer(h2, u_bf, vt_bf, cnt, e1, rank2, e2, g, b, alpha, heads, tt, te):
    n, d = h2.shape
    assert te % PEER_CHUNK == 0
    blk = pl.BlockSpec((heads, N_KEYS, tt), lambda i, e: (0, 0, i))
    return pl.pallas_call(
        functools.partial(_peer_kernel, alpha, heads),
        grid=(n // tt, u_bf.shape[0] // te),
        in_specs=[pl.BlockSpec((tt, d), lambda i, e: (i, 0)),
                  pl.BlockSpec((te, d), lambda i, e: (e, 0)),
                  pl.BlockSpec((d, te), lambda i, e: (0, e)),
                  blk, blk, blk, blk, _full((1, d)), _full((1, d))],
        out_specs=pl.BlockSpec((tt, d), lambda i, e: (i, 0)),
        out_shape=jax.ShapeDtypeStruct((n, d), F32),
        scratch_shapes=[pltpu.VMEM((d, tt), F32), pltpu.VMEM((d, tt), BF16)],
        compiler_params=_cparams("parallel", "arbitrary"),
        name="peer",
    )(h2, u_bf, vt_bf, cnt, e1, rank2, e2, g, b)


def _block_diag2(w):
    z = jnp.zeros_like(w[0])
    return jnp.concatenate([jnp.concatenate([w[0], z], axis=1), jnp.concatenate([z, w[1]], axis=1)], axis=0)


def _layer(h_in, mem2, batch, seq, mem_len, lng, lnb, p):
    n, d = h_in.shape
    (w_in, rwkv_mu, rwkv_w0, rwkv_w2, rwkv_a0, rwkv_a2, rwkv_g2, rwkv_k_k, rwkv_k_a, rwkv_r_k,
     rwkv_gn_g, rwkv_gn_b, gmlp_ln_g, gmlp_ln_b, gmlp_w_s, gmlp_b_s, w_branch, w_mix_out, ln1_g, ln1_b,
     mem_ln_g, mem_ln_b, xattn_w_q, xattn_w_kv, xattn_w_o, ln2_g, ln2_b,
     peer_w_query, peer_sub_keys, peer_u, peer_v, ln3_g, ln3_b, alpha) = p
    width = rwkv_k_k.shape[0]
    rw_cols = rwkv_mu.shape[0]
    gm_cols = 2 * gmlp_ln_g.shape[0]
    chunk = gmlp_w_s.shape[1]
    row1 = lambda t: t.reshape(1, -1)

    tm = TOKEN_TILE
    wa = w_in[:, :rw_cols].astype(BF16)
    wb = w_in[:, rw_cols:rw_cols + gm_cols].astype(BF16)
    wg = w_in[:, rw_cols + gm_cols:].astype(BF16)
    pa, pb, pg = _inproj(h_in, row1(lng), row1(lnb), wa, wb, wg, tm)

    heads_r = width // HEAD_DIM
    bd = jnp.asarray(np.kron(np.eye(heads_r, dtype=np.float32), np.ones((HEAD_DIM, HEAD_DIM), np.float32)), BF16)
    r, v, na, lw, kd, bv, g, bonus = _rwkv_pre(
        pa, seq, row1(rwkv_mu), row1(rwkv_w0), _hilo(_block_diag2(rwkv_w2)), row1(rwkv_a0),
        _hilo(_block_diag2(rwkv_a2)), _hilo(rwkv_g2), row1(rwkv_k_k), row1(rwkv_k_a), row1(rwkv_r_k), bd, tm)
    yf, yb = _scan(r, v, na, lw, kd, bv, batch, seq)

    groups = gmlp_w_s.shape[0]
    wscat = jnp.concatenate([gmlp_w_s[0::2], gmlp_w_s[1::2]], axis=2).astype(BF16)
    bsfull = jnp.repeat(gmlp_b_s.T, HEAD_DIM, axis=1)
    assert groups * HEAD_DIM == width
    h1 = _mixer(h_in, yf, yb, g, bonus, pb, pg, row1(lng), row1(lnb), row1(rwkv_gn_g), row1(rwkv_gn_b), bd,
                row1(gmlp_ln_g), row1(gmlp_ln_b), wscat, bsfull, w_branch.astype(BF16), w_mix_out.astype(BF16),
                row1(ln1_g), row1(ln1_b), alpha, chunk, tm)

    kv = _memkv(mem2, row1(mem_ln_g), row1(mem_ln_b), xattn_w_kv.astype(BF16), mem_len)
    h2 = _xattn(h1, kv, xattn_w_q.astype(BF16), xattn_w_o.astype(BF16), row1(ln2_g), row1(ln2_b),
                alpha, XATTN_HEADS, seq, mem_len, tm)

    pheads = peer_w_query.shape[1] // (2 * LANES)
    cnt, e1, rank2, e2 = _route(h2, peer_w_query, peer_sub_keys, pheads, TOKEN_TILE)
    h3 = _peer(h2, peer_u.astype(BF16), (0.5 * peer_v).astype(BF16).T, cnt, e1, rank2, e2, row1(ln3_g), row1(ln3_b),
               alpha, pheads, TOKEN_TILE, PEER_EXPERT_TILE)
    return h3


def kernel(x, mem, ln_emb_g, ln_emb_b, w_in, rwkv_mu, rwkv_w0, rwkv_w2, rwkv_a0, rwkv_a2, rwkv_g2, rwkv_k_k, rwkv_k_a, rwkv_r_k, rwkv_gn_g, rwkv_gn_b, gmlp_ln_g, gmlp_ln_b, gmlp_w_s, gmlp_b_s, w_branch, w_mix_out, ln1_g, ln1_b, mem_ln_g, mem_ln_b, xattn_w_q, xattn_w_kv, xattn_w_o, ln2_g, ln2_b, peer_w_query, peer_sub_keys, peer_u, peer_v, ln3_g, ln3_b):
    batch, seq, d = x.shape
    mem_len = mem.shape[1]
    depth = w_in.shape[0]
    assert depth == 1, "the layer pipeline applies the embedding LN inside the first layer only"
    alpha = float((2.0 * depth) ** 0.25)
    x2 = x.reshape(batch * seq, d)
    mem2 = mem.reshape(batch * mem_len, d)
    flat = lambda t: t.reshape(-1)
    p = (w_in[0], rwkv_mu[0], flat(rwkv_w0[0]), rwkv_w2[0], flat(rwkv_a0[0]), rwkv_a2[0], rwkv_g2[0],
         rwkv_k_k[0], rwkv_k_a[0], flat(rwkv_r_k[0]), rwkv_gn_g[0], rwkv_gn_b[0], gmlp_ln_g[0], gmlp_ln_b[0],
         gmlp_w_s[0], gmlp_b_s[0], w_branch[0], w_mix_out[0], ln1_g[0], ln1_b[0], mem_ln_g[0], mem_ln_b[0],
         xattn_w_q[0], xattn_w_kv[0], xattn_w_o[0], ln2_g[0], ln2_b[0], peer_w_query[0], peer_sub_keys[0],
         peer_u[0], peer_v[0], ln3_g[0], ln3_b[0], alpha)
    h = _layer(x2, mem2, batch, seq, mem_len, ln_emb_g, ln_emb_b, p)
    return h.reshape(batch, seq, d)
```

```python
import functools

import numpy as np
import jax
import jax.numpy as jnp
from jax import lax
from jax.experimental import pallas as pl
from jax.experimental.pallas import tpu as pltpu

F32 = jnp.float32
BF16 = jnp.bfloat16
HIGHEST = lax.Precision.HIGHEST

LN_EPS = 1e-5
GN_EPS = 64e-5
LANES = 128
HEAD_DIM = 64
SCAN_CHUNK = 64
TOPK = 16
N_KEYS = 128
XATTN_HEADS = 4
TOKEN_TILE = 256
PEER_EXPERT_TILE = 2048
VMEM_LIMIT = 56 * 1024 * 1024


def _cparams(*sem):
    return pltpu.CompilerParams(dimension_semantics=sem, vmem_limit_bytes=VMEM_LIMIT)


def _ln(x, g, b, eps=LN_EPS):
    mu = jnp.mean(x, axis=-1, keepdims=True)
    xc = x - mu
    var = jnp.mean(xc * xc, axis=-1, keepdims=True)
    return xc * lax.rsqrt(var + eps) * g + b


def _mm(a, b):
    return jnp.dot(a.astype(BF16), b.astype(BF16), preferred_element_type=F32)


def _mm_nt(a, b):
    return lax.dot_general(a, b, (((1,), (1,)), ((), ())), preferred_element_type=F32)


def _mm_nt_hi(a, b):
    return lax.dot_general(a, b, (((1,), (1,)), ((), ())), precision=HIGHEST,
                           preferred_element_type=F32)


def _split_bf16(x):
    hi = x.astype(BF16)
    return hi, (x - hi.astype(F32)).astype(BF16)


def _hilo(w):
    hi, lo = _split_bf16(w)
    return jnp.stack([hi, lo])


def _mm_x3(x, w_ref):
    hi, lo = _split_bf16(x)
    w_hi = w_ref[0]
    return (jnp.dot(hi, w_hi, preferred_element_type=F32) + jnp.dot(hi, w_ref[1], preferred_element_type=F32)
            + jnp.dot(lo, w_hi, preferred_element_type=F32))


def _segsum(x, ones):
    hi, lo = _split_bf16(x)
    return jnp.dot(hi, ones, preferred_element_type=F32) + jnp.dot(lo, ones, preferred_element_type=F32)


def _gelu_x2(x):
    return x * (1.0 + lax.erf(x * np.float32(1.0 / np.sqrt(2.0))))


def _gelu(x):
    return 0.5 * _gelu_x2(x)


def _sigmoid(x):
    return 1.0 / (1.0 + jnp.exp(-x))


def _full(shape):
    nd = len(shape)
    return pl.BlockSpec(shape, lambda *_: (0,) * nd)


def _inproj_kernel(x_ref, g_ref, b_ref, wa_ref, wb_ref, wg_ref, pa_ref, pb_ref, pg_ref):
    h = _ln(x_ref[...], g_ref[...], b_ref[...]).astype(BF16)
    pa_ref[...] = jnp.dot(h, wa_ref[...], preferred_element_type=F32)
    pb_ref[...] = jnp.dot(h, wb_ref[...], preferred_element_type=F32)
    pg_ref[...] = jnp.dot(h, wg_ref[...], preferred_element_type=F32)


def _inproj(x2, g, b, wa, wb, wg, tm):
    n, d = x2.shape
    ca, cb, cg = wa.shape[1], wb.shape[1], wg.shape[1]
    row = lambda c: pl.BlockSpec((tm, c), lambda i: (i, 0))
    return pl.pallas_call(
        _inproj_kernel,
        grid=(n // tm,),
        in_specs=[row(d), _full((1, d)), _full((1, d)), _full((d, ca)), _full((d, cb)), _full((d, cg))],
        out_specs=[row(ca), row(cb), row(cg)],
        out_shape=[jax.ShapeDtypeStruct((n, c), F32) for c in (ca, cb, cg)],
        compiler_params=_cparams("parallel"),
        name="inproj",
    )(x2, g, b, wa, wb, wg)


def _rwkv_pre_kernel(tiles_per_seq, width,
                     p_ref, prev_ref, next_ref, mu_ref, w0_ref, w2_ref, a0_ref, a2_ref, g2_ref,
                     kk_ref, ka_ref, rk_ref, bd_ref,
                     r_out, v_out, na_out, lw_out, kd_out, bv_out, g_out, bonus_out):
    i = pl.program_id(0)
    pos = i % tiles_per_seq
    p = p_ref[...]
    tm = p.shape[0]
    row = lax.broadcasted_iota(jnp.int32, (tm, 1), 0)
    prev_row = jnp.where(pos == 0, 0.0, prev_ref[7:8, :])
    next_row = jnp.where(pos == tiles_per_seq - 1, 0.0, next_ref[0:1, :])
    prev = jnp.where(row == 0, prev_row, pltpu.roll(p, 1, 0))
    nxt = jnp.where(row == tm - 1, next_row, pltpu.roll(p, tm - 1, 0))
    ps = p + mu_ref[...] * (0.5 * (prev + nxt) - p)

    w = width
    r = ps[:, 0:w]
    k = ps[:, w:2 * w]
    v = ps[:, 2 * w:3 * w]
    wd = ps[:, 3 * w:3 * w + LANES]
    ad = ps[:, 3 * w + LANES:3 * w + 2 * LANES]
    gd = ps[:, 3 * w + 2 * LANES:3 * w + 3 * LANES]

    wz = w0_ref[...] + _mm_x3(jnp.tanh(wd), w2_ref)
    log_decay = np.float32(-np.exp(-0.5)) * _sigmoid(wz)
    a = _sigmoid(a0_ref[...] + _mm_x3(ad, a2_ref))
    g = _mm_x3(_sigmoid(gd), g2_ref)

    bd = bd_ref[...]
    kk = k * kk_ref[...]
    nrm = jnp.sqrt(_segsum(kk * kk, bd))
    kk = kk / jnp.maximum(nrm, 1e-12)

    ka = ka_ref[...]
    k0 = k * (1.0 + (a[:, 0:w] - 1.0) * ka)
    k1 = k * (1.0 + (a[:, w:2 * w] - 1.0) * ka)
    rk = _segsum(r * (k0 + k1) * rk_ref[...], bd)

    r_out[...] = r
    v_out[...] = v
    na_out[...] = -kk
    lw_out[0] = log_decay[:, 0:w]
    lw_out[1] = log_decay[:, w:2 * w]
    kd_out[0] = k0
    kd_out[1] = k1
    bv_out[0] = kk * a[:, 0:w]
    bv_out[1] = kk * a[:, w:2 * w]
    g_out[...] = g
    bonus_out[...] = rk * v


def _rwkv_pre(pa, seq, mu, w0, w2bd, a0, a2bd, g2, k_k, k_a, r_k, bd, tm):
    n, ca = pa.shape
    w = bd.shape[0]
    tps = seq // tm
    hb = tm // 8
    nb8 = n // 8
    row = lambda c: pl.BlockSpec((tm, c), lambda i: (i, 0))
    row2 = pl.BlockSpec((2, tm, w), lambda i: (0, i, 0))
    outs = [jax.ShapeDtypeStruct((n, w), F32)] * 3 + [jax.ShapeDtypeStruct((2, n, w), F32)] * 3 + \
           [jax.ShapeDtypeStruct((n, w), F32)] * 2
    return pl.pallas_call(
        functools.partial(_rwkv_pre_kernel, tps, w),
        grid=(n // tm,),
        in_specs=[row(ca),
                  pl.BlockSpec((8, ca), lambda i: (jnp.maximum(i * hb - 1, 0), 0)),
                  pl.BlockSpec((8, ca), lambda i: (jnp.minimum((i + 1) * hb, nb8 - 1), 0)),
                  _full((1, ca)), _full((1, 2 * w)), _full(w2bd.shape), _full((1, 2 * w)),
                  _full(a2bd.shape), _full(g2.shape), _full((1, w)), _full((1, w)), _full((1, w)),
                  _full((w, w))],
        out_specs=[row(w), row(w), row(w), row2, row2, row2, row(w), row(w)],
        out_shape=outs,
        compiler_params=_cparams("parallel"),
        name="rwkv_pre",
    )(pa, pa, pa, mu, w0, w2bd, a0, a2bd, g2, k_k, k_a, r_k, bd)


def _scan_kernel(rf_ref, vf_ref, af_ref, rb_ref, vb_ref, ab_ref, lwf_ref, kf_ref, bf_ref, lwb_ref, kb_ref, bb_ref,
                 ms_ref, mi_ref, lc_ref, yf_ref, yb_ref, s_scr):
    c = SCAN_CHUNK
    h2 = 2 * c

    @pl.when(pl.program_id(1) == 0)
    def _():
        s_scr[...] = jnp.zeros_like(s_scr)

    first_head = lax.broadcasted_iota(jnp.int32, (1, LANES), 1) < HEAD_DIM

    def stack(x):
        return jnp.concatenate([jnp.where(first_head, x, 0.0), jnp.where(first_head, 0.0, x)], axis=0)

    dot = functools.partial(jnp.dot, preferred_element_type=F32)
    n_groups = lwf_ref.shape[1] // LANES
    sls = [slice(m * LANES, (m + 1) * LANES) for m in range(n_groups)]

    ar, bk, v_s, bk_e, p_end, strict, incl = [], [], [], [], [], [], []
    dirs = ((rf_ref, vf_ref, af_ref, lwf_ref, kf_ref, bf_ref), (rb_ref, vb_ref, ab_ref, lwb_ref, kb_ref, bb_ref))
    for d, (r_ref, v_ref, a_ref, lw_ref, k_ref, b_ref) in enumerate(dirs):
        lw = lw_ref[...]
        l1 = lw.astype(BF16)
        rest = lw - l1.astype(F32)
        l2 = rest.astype(BF16)
        l3 = (rest - l2.astype(F32)).astype(BF16)
        lc = lc_ref[d]
        cum = dot(lc, l1) + dot(lc, l2) + dot(lc, l3)
        tot = jnp.sum(lw, axis=0, keepdims=True)
        e_neg = jnp.exp(-cum)
        e_end = jnp.exp(tot - cum)
        a_t = a_ref[...] * jnp.exp(cum - lw)
        r_t = r_ref[...] * jnp.exp(cum)
        b_in = b_ref[...]
        k_in = k_ref[...]
        b_t = b_in * e_neg
        k_t = k_in * e_neg
        b_e = b_in * e_end
        k_e = k_in * e_end
        v_in = v_ref[...]
        pe = jnp.exp(tot)
        sd = ms_ref[d] > 0.0
        ic = mi_ref[d] > 0.0
        for s in sls:
            ar.append(jnp.concatenate([stack(a_t[:, s]), stack(r_t[:, s])], axis=0).astype(BF16))
            bk.append(jnp.concatenate([stack(b_t[:, s]), stack(k_t[:, s])], axis=0).astype(BF16))
            bk_e.append(jnp.concatenate([stack(b_e[:, s]), stack(k_e[:, s])], axis=0).astype(BF16))
            v_s.append(stack(v_in[:, s]))
            p_end.append(pe[:, s])
            strict.append(sd)
            incl.append(ic)
    gs = range(2 * n_groups)
    v_b = [x.astype(BF16) for x in v_s]
    state = [s_scr[m] for m in gs]

    l_all = [_mm_nt(ar[m], bk[m]) for m in gs]
    from_state = [_mm_nt(ar[m], state[m].astype(BF16)) for m in gs]
    l_ab = [jnp.where(strict[m], l_all[m][:h2, :h2], 0.0).astype(BF16) for m in gs]
    l_k = [jnp.concatenate([jnp.where(strict[m], l_all[m][:h2, h2:], 0.0),
                            jnp.where(incl[m], l_all[m][h2:, h2:], 0.0)], axis=0).astype(BF16) for m in gs]
    l_rb = [jnp.where(incl[m], l_all[m][h2:, :h2], 0.0).astype(BF16) for m in gs]
    from_v = [dot(l_k[m], v_b[m]) for m in gs]

    x = [from_state[m][:h2] + from_v[m][:h2] for m in gs]
    lp = l_ab
    x = [x[m] + dot(lp[m], x[m].astype(BF16)) for m in gs]
    span = 2
    while span < c:
        lp = [dot(lp[m], lp[m]).astype(BF16) for m in gs]
        x = [x[m] + dot(lp[m], x[m].astype(BF16)) for m in gs]
        span *= 2
    u_s = x

    y_s = [from_state[m][h2:] + from_v[m][h2:] + dot(l_rb[m], u_s[m].astype(BF16)) for m in gs]
    uv_t = [jnp.transpose(jnp.concatenate([u_s[m], v_s[m]], axis=0)).astype(BF16) for m in gs]
    new_state = [state[m] * p_end[m] + dot(uv_t[m], bk_e[m]) for m in gs]
    for m in gs:
        y_ref = yf_ref if m < n_groups else yb_ref
        y_ref[:, sls[m % n_groups]] = y_s[m][:c] + y_s[m][c:]
        s_scr[m] = new_state[m]


def _scan(r, v, na, lw, kd, bv, batch, seq):
    n, w = r.shape
    c = SCAN_CHUNK
    nc = seq // c
    t = np.arange(c)
    before = [(t[None, :] < t[:, None]), (t[None, :] > t[:, None])]
    eye = np.eye(c, dtype=bool)
    tile2 = lambda m: np.tile(m, (2, 2))
    ms = jnp.asarray(np.stack([tile2(m) for m in before]).astype(np.float32))
    mi = jnp.asarray(np.stack([tile2(m | eye) for m in before]).astype(np.float32))
    lc = jnp.asarray(np.stack([(m | eye) for m in before]).astype(np.float32), BF16)

    fwd = pl.BlockSpec((c, w), lambda b, ci: (b * nc + ci, 0))
    bwd = pl.BlockSpec((c, w), lambda b, ci: (b * nc + nc - 1 - ci, 0))
    fwd_d = pl.BlockSpec((None, c, w), lambda b, ci: (0, b * nc + ci, 0))
    bwd_d = pl.BlockSpec((None, c, w), lambda b, ci: (1, b * nc + nc - 1 - ci, 0))
    return pl.pallas_call(
        _scan_kernel,
        grid=(batch, nc),
        in_specs=[fwd, fwd, fwd, bwd, bwd, bwd, fwd_d, fwd_d, fwd_d, bwd_d, bwd_d, bwd_d,
                  _full(ms.shape), _full(mi.shape), _full(lc.shape)],
        out_specs=[fwd, bwd],
        out_shape=[jax.ShapeDtypeStruct((n, w), F32)] * 2,
        scratch_shapes=[pltpu.VMEM((2 * (w // LANES), LANES, LANES), F32)],
        compiler_params=_cparams("parallel", "arbitrary"),
        name="scan",
    )(r, v, na, r, v, na, lw, kd, bv, lw, kd, bv, ms, mi, lc)


def _mixer_kernel(alpha, chunk,
                  x_ref, yf_ref, yb_ref, g_ref, bonus_ref, pb_ref, pg_ref,
                  lng_ref, lnb_ref, gng_ref, gnb_ref, bd_ref, mlg_ref, mlb_ref, ws_ref, bs_ref,
                  wbr_ref, wmix_ref, l1g_ref, l1b_ref, o_ref):
    h0 = _ln(x_ref[...], lng_ref[...], lnb_ref[...])
    bd = bd_ref[...]
    inv = np.float32(1.0 / HEAD_DIM)

    y = yf_ref[...] + yb_ref[...]
    mu = _segsum(y, bd) * inv
    yc = y - mu
    var = _segsum(yc * yc, bd) * inv
    y_a = (yc * lax.rsqrt(var + GN_EPS) * gng_ref[...] + gnb_ref[...] + bonus_ref[...]) * g_ref[...]

    gp = _gelu(pb_ref[...])
    w = gp.shape[1] // 2
    u = gp[:, :w]
    vn = _ln(gp[:, w:], mlg_ref[...], mlb_ref[...])
    tm = u.shape[0]
    first_group = lax.broadcasted_iota(jnp.int32, (1, LANES), 1) < HEAD_DIM
    rows = []
    for ci in range(tm // chunk):
        vc = vn[ci * chunk:(ci + 1) * chunk]
        cols = []
        for m in range(w // LANES):
            vp = vc[:, m * LANES:(m + 1) * LANES]
            stacked = jnp.concatenate([jnp.where(first_group, vp, 0.0), jnp.where(first_group, 0.0, vp)], axis=0)
            cols.append(_mm(ws_ref[m], stacked))
        rows.append(jnp.concatenate(cols, axis=1) + bs_ref[...])
    sv = jnp.concatenate(rows, axis=0)
    y_b = u * sv

    pg = pg_ref[...]
    d = pg.shape[1] // 2
    merged = _sigmoid(pg[:, :d]) * _mm(y_a, wbr_ref[0]) + _sigmoid(pg[:, d:]) * _mm(y_b, wbr_ref[1])
    mix = _mm(merged, wmix_ref[...])
    o_ref[...] = _ln(alpha * h0 + mix, l1g_ref[...], l1b_ref[...])


def _mixer(x2, yf, yb, g, bonus, pb, pg, lng, lnb, gng, gnb, bd, mlg, mlb, wscat, bsfull, wbr, wmix, l1g, l1b,
           alpha, chunk, tm):
    n, d = x2.shape
    w = g.shape[1]
    row = lambda c: pl.BlockSpec((tm, c), lambda i: (i, 0))
    return pl.pallas_call(
        functools.partial(_mixer_kernel, alpha, chunk),
        grid=(n // tm,),
        in_specs=[row(d), row(w), row(w), row(w), row(w), row(2 * w), row(2 * d),
                  _full((1, d)), _full((1, d)), _full((1, w)), _full((1, w)), _full((w, w)),
                  _full((1, w)), _full((1, w)), _full(wscat.shape), _full(bsfull.shape),
                  _full(wbr.shape), _full(wmix.shape), _full((1, d)), _full((1, d))],
        out_specs=row(d),
        out_shape=jax.ShapeDtypeStruct((n, d), F32),
        compiler_params=_cparams("parallel"),
        name="mixer",
    )(x2, yf, yb, g, bonus, pb, pg, lng, lnb, gng, gnb, bd, mlg, mlb, wscat, bsfull, wbr, wmix, l1g, l1b)


def _memkv_kernel(m_ref, g_ref, b_ref, w_ref, o_ref):
    o_ref[...] = _mm(_ln(m_ref[...], g_ref[...], b_ref[...]), w_ref[...]).astype(BF16)


def _memkv(mem2, g, b, wkv, tm):
    n, d = mem2.shape
    return pl.pallas_call(
        _memkv_kernel,
        grid=(n // tm,),
        in_specs=[pl.BlockSpec((tm, d), lambda i: (i, 0)), _full((1, d)), _full((1, d)), _full(wkv.shape)],
        out_specs=pl.BlockSpec((tm, wkv.shape[1]), lambda i: (i, 0)),
        out_shape=jax.ShapeDtypeStruct((n, wkv.shape[1]), BF16),
        compiler_params=_cparams("parallel"),
        name="memkv",
    )(mem2, g, b, wkv)


def _xattn_kernel(alpha, heads, h_ref, kv_ref, wq_ref, wo_ref, g_ref, b_ref, o_ref):
    h = h_ref[...]
    d = h.shape[1]
    hd = d // heads
    q = _mm(h, wq_ref[...])
    kv = kv_ref[...]
    scale = np.float32(hd ** -0.5)
    hs = range(heads)
    qb = q.astype(BF16)
    s = [_mm_nt(qb[:, i * hd:(i + 1) * hd], kv[:, i * hd:(i + 1) * hd]) * scale for i in hs]
    s = [x - jnp.max(x, axis=-1, keepdims=True) for x in s]
    e = [jnp.exp(x) for x in s]
    prob = [x / jnp.sum(x, axis=-1, keepdims=True) for x in e]
    o = [_mm(prob[i], kv[:, d + i * hd:d + (i + 1) * hd]) for i in hs]
    out = _mm(jnp.concatenate(o, axis=1), wo_ref[...])
    o_ref[...] = _ln(alpha * h + out, g_ref[...], b_ref[...])


def _xattn(h1, kv, wq, wo, g, b, alpha, heads, seq, mem_len, tm):
    n, d = h1.shape
    tps = seq // tm
    return pl.pallas_call(
        functools.partial(_xattn_kernel, alpha, heads),
        grid=(n // tm,),
        in_specs=[pl.BlockSpec((tm, d), lambda i: (i, 0)),
                  pl.BlockSpec((mem_len, 2 * d), lambda i: (i // tps, 0)),
                  _full(wq.shape), _full(wo.shape), _full((1, d)), _full((1, d))],
        out_specs=pl.BlockSpec((tm, d), lambda i: (i, 0)),
        out_shape=jax.ShapeDtypeStruct((n, d), F32),
        compiler_params=_cparams("parallel"),
        name="xattn",
    )(h1, kv, wq, wo, g, b)


def _extract_topk(s, pos, k):
    rank = jnp.full(s.shape, float(k), F32)
    vals = []
    for j in range(k):
        m = jnp.max(s, axis=0, keepdims=True)
        first = jnp.min(jnp.where(s == m, pos, np.float32(np.inf)), axis=0, keepdims=True)
        sel = pos == first
        rank = jnp.where(sel, float(j), rank)
        s = jnp.where(sel, -jnp.inf, s)
        vals.append(m)
    return vals, rank


def _candidate_layout(k):
    rest = ((2, 2), (3, 2), (4, 2), (2, 3), (3, 3), (2, 4), (15, 15), (15, 15))
    pieces = [("row", 0, 0, 16), ("row", 1, 0, 8), ("col", 0, 0, 16), ("col", 1, 0, 8), ("pairs", rest, 0, 8)]
    pos, seen = [], set()
    for kind, fixed, start, n in pieces:
        for o in range(start, start + n):
            i, j = fixed[o] if kind == "pairs" else (fixed, o) if kind == "row" else (o, fixed)
            ok = (i + 1) * (j + 1) <= k and (i, j) not in seen
            if ok:
                seen.add((i, j))
            pos.append(float(i * k + j) if ok else -1.0)
    assert len(seen) == sum(k // (i + 1) for i in range(k))
    return pieces, np.asarray(pos, np.float32)


def _sort16_comparators():
    def merge(lo, hi, r):
        step = r * 2
        if step < hi - lo:
            yield from merge(lo, hi, step)
            yield from merge(lo + r, hi, step)
            yield from ((i, i + r) for i in range(lo + r, hi - r, step))
        else:
            yield (lo, lo + r)

    def sort(lo, hi):
        if hi - lo >= 1:
            mid = lo + (hi - lo) // 2
            yield from sort(lo, mid)
            yield from sort(mid + 1, hi)
            yield from merge(lo, hi, 1)

    return tuple(sort(0, 15))


def _top16_sorted(s):
    def exchange(v, i, j):
        v[i], v[j] = jnp.maximum(v[i], v[j]), jnp.minimum(v[i], v[j])

    v = [s[8 * r:8 * r + 8] for r in range(16)]
    for i, j in _sort16_comparators():
        exchange(v, i, j)
    for shift in (4, 2, 1):
        w = [pltpu.roll(x, shift, 0) for x in v]
        v = [jnp.maximum(v[i], w[15 - i]) for i in range(16)]
        for stride in (8, 4, 2, 1):
            for i in range(16):
                if not i & stride:
                    exchange(v, i, i + stride)
    return v


def _route_kernel(heads, h_ref, wq_ref, sk_ref, cpos_ref, cnt_ref, e1_ref, rank2_ref, e2_ref, q_scr):
    h = h_ref[...]
    tt = h.shape[0]
    q_scr[...] = _mm(h, wq_ref[...])
    k = TOPK
    pieces, _ = _candidate_layout(k)
    cpos = cpos_ref[...]
    cvalid = cpos >= 0.0
    cpos_inf = jnp.where(cvalid, cpos, np.float32(np.inf))
    key_pos = lax.broadcasted_iota(jnp.int32, (N_KEYS, tt), 0).astype(F32)
    row16 = lax.broadcasted_iota(jnp.int32, (k, 1), 0)

    def scores(hd):
        off = pl.multiple_of(hd * (2 * LANES), 2 * LANES)
        s1 = _mm_nt_hi(sk_ref[0], q_scr[:, pl.ds(off, LANES)])
        s2 = _mm_nt_hi(sk_ref[1], q_scr[:, pl.ds(off + LANES, LANES)])
        return s1, s2

    def pair_selection(top1, top2):
        t1 = jnp.concatenate(top1, axis=0)
        t2 = jnp.concatenate(top2, axis=0)
        parts = []
        for kind, fixed, start, n in pieces:
            if kind == "row":
                parts.append(top1[fixed] + t2[start:start + n])
            elif kind == "col":
                parts.append(t1[start:start + n] + top2[fixed])
            else:
                parts.append(jnp.concatenate([top1[i] + top2[j] for i, j in fixed], axis=0))
        cand = jnp.where(cvalid, jnp.concatenate(parts, axis=0), -jnp.inf)
        best, crank = _extract_topk(cand, cpos_inf, k)
        chosen = jnp.where(crank < float(k), 1.0, 0.0)
        n16 = jnp.zeros((k, tt), F32)
        r0 = 0
        for kind, fixed, start, n in pieces:
            c = chosen[r0:r0 + n]
            r0 += n
            if kind == "row":
                n16 = n16 + jnp.where(row16 == fixed, jnp.sum(c, axis=0, keepdims=True), 0.0)
            elif kind == "pairs":
                for o, (i, _) in enumerate(fixed):
                    n16 = n16 + jnp.where(row16 == i, c[o:o + 1], 0.0)
            elif n == k:
                n16 = n16 + c
            else:
                n16 = n16 + jnp.concatenate([c, jnp.zeros((k - n, tt), F32)], axis=0)
        return best, n16

    def emit(hd, s1, s2, max1, max2, best, cnt_dense, rank2):
        z = jnp.zeros_like(best[0])
        for j in range(k):
            z = z + jnp.exp(best[j] - best[0])
        cnt_ref[hd] = cnt_dense
        e1_ref[hd] = jnp.exp(s1 - max1) / (2.0 * z)
        rank2_ref[hd] = rank2.astype(BF16)
        e2_ref[hd] = jnp.exp(s2 - max2).astype(BF16)

    def head_by_sorting(hd, bad):
        s1, s2 = scores(hd)
        v1 = _top16_sorted(s1)
        v2 = _top16_sorted(s2)
        top1 = [x[0:1] for x in v1]
        top2 = [x[0:1] for x in v2]
        best, n16 = pair_selection(top1, top2)
        rank2 = jnp.full_like(s2, float(k))
        cnt_dense = jnp.zeros_like(s1)
        for i in reversed(range(k)):
            rank2 = jnp.where(s2 >= top2[i], float(i), rank2)
            cnt_dense = jnp.where(s1 == top1[i], n16[i:i + 1], cnt_dense)
        in1 = jnp.sum(jnp.where(s1 >= top1[k - 1], 1.0, 0.0), axis=0, keepdims=True)
        in2 = jnp.sum(jnp.where(rank2 < float(k), 1.0, 0.0), axis=0, keepdims=True)
        ties = jnp.where(in1 == float(k), 0.0, 1.0) + jnp.where(in2 == float(k), 0.0, 1.0)
        for i in range(k - 1):
            ties = ties + jnp.where(top1[i] > top1[i + 1], 0.0, 1.0) + jnp.where(top2[i] > top2[i + 1], 0.0, 1.0)
        emit(hd, s1, s2, top1[0], top2[0], best, cnt_dense, rank2)
        return bad + ties

    def head_exact(hd, carry):
        s1, s2 = scores(hd)
        top1, rank1 = _extract_topk(s1, key_pos, k)
        top2, rank2 = _extract_topk(s2, key_pos, k)
        best, n16 = pair_selection(top1, top2)
        cnt_dense = jnp.zeros_like(s1)
        for i in range(k):
            cnt_dense = cnt_dense + jnp.where(rank1 == float(i), n16[i:i + 1], 0.0)
        emit(hd, s1, s2, top1[0], top2[0], best, cnt_dense, rank2)
        return carry

    bad = lax.fori_loop(0, heads, head_by_sorting, jnp.zeros((1, tt), F32))

    @pl.when(jnp.sum(bad) > 0.0)
    def _():
        lax.fori_loop(0, heads, head_exact, 0)


def _route(h2, wq, sk, heads, tt):
    n, d = h2.shape
    _, pos = _candidate_layout(TOPK)
    cpos = jnp.asarray(np.broadcast_to(pos[:, None], (pos.shape[0], tt)).copy())
    blk = pl.BlockSpec((heads, N_KEYS, tt), lambda i: (0, 0, i))
    return pl.pallas_call(
        functools.partial(_route_kernel, heads),
        grid=(n // tt,),
        in_specs=[pl.BlockSpec((tt, d), lambda i: (i, 0)), _full(wq.shape), _full(sk.shape), _full(cpos.shape)],
        out_specs=[blk] * 4,
        out_shape=[jax.ShapeDtypeStruct((heads, N_KEYS, n), F32)] * 2
                  + [jax.ShapeDtypeStruct((heads, N_KEYS, n), BF16)] * 2,
        scratch_shapes=[pltpu.VMEM((tt, wq.shape[1]), F32)],
        compiler_params=_cparams("parallel"),
        name="route",
    )(h2, wq.astype(BF16), sk, cpos)


PEER_CHUNK = 256


def _peer_kernel(alpha, heads, h_ref, u_ref, vt_ref, cnt_ref, e1_ref, rank2_ref, e2_ref, g_ref, b_ref, o_ref,
                 acc_scr, hbt_scr):
    e = pl.program_id(1)
    te = u_ref.shape[0]
    tt = h_ref.shape[0]
    sub = 16
    grp = N_KEYS // sub
    per_chunk = PEER_CHUNK // N_KEYS
    dot = functools.partial(jnp.dot, preferred_element_type=F32)

    @pl.when(e == 0)
    def _():
        acc_scr[...] = jnp.zeros_like(acc_scr)
        hbt_scr[...] = jnp.transpose(h_ref[...]).astype(BF16)

    for c in range(te // PEER_CHUNK):
        rows = slice(c * PEER_CHUNK, (c + 1) * PEER_CHUNK)
        act = dot(u_ref[rows, :], hbt_scr[...])
        ws = []
        for bl in range(per_chunk):
            il = c * per_chunk + bl
            r = il % 8
            base = pl.multiple_of(e * (te // N_KEYS) + (il - r), 8)
            gate = jnp.zeros((grp, sub, tt), BF16)
            for hd in range(heads):
                cnt = jnp.broadcast_to(cnt_ref[hd, pl.ds(base, 8), :][r:r + 1], (sub, tt)).astype(BF16)
                e1 = jnp.broadcast_to(e1_ref[hd, pl.ds(base, 8), :][r:r + 1], (sub, tt)).astype(BF16)
                rank2 = rank2_ref[hd].reshape(grp, sub, tt)
                e2 = e2_ref[hd].reshape(grp, sub, tt)
                gate = gate + jnp.where(rank2 < cnt[None], e2, jnp.zeros_like(e2)) * e1[None]
            a = act[bl * N_KEYS:(bl + 1) * N_KEYS]
            ws.append((gate * _gelu_x2(a).astype(BF16).reshape(grp, sub, tt)).reshape(N_KEYS, tt))
        acc_scr[...] += dot(vt_ref[:, rows], jnp.concatenate(ws, axis=0))

    @pl.when(e == pl.num_programs(1) - 1)
    def _():
        y = jnp.transpose(acc_scr[...])
        o_ref[...] = _ln(alpha * h_ref[...] + y, g_ref[...], b_ref[...])


def _peer(h2, u_bf, vt_bf, cnt, e1, rank2, e2, g, b, alpha, heads, tt, te):
    n, d = h2.shape
    assert te % PEER_CHUNK == 0
    blk = pl.BlockSpec((heads, N_KEYS, tt), lambda i, e: (0, 0, i))
    return pl.pallas_call(
        functools.partial(_peer_kernel, alpha, heads),
        grid=(n // tt, u_bf.shape[0] // te),
        in_specs=[pl.BlockSpec((tt, d), lambda i, e: (i, 0)),
                  pl.BlockSpec((te, d), lambda i, e: (e, 0)),
                  pl.BlockSpec((d, te), lambda i, e: (0, e)),
                  blk, blk, blk, blk, _full((1, d)), _full((1, d))],
        out_specs=pl.BlockSpec((tt, d), lambda i, e: (i, 0)),
        out_shape=jax.ShapeDtypeStruct((n, d), F32),
        scratch_shapes=[pltpu.VMEM((d, tt), F32), pltpu.VMEM((d, tt), BF16)],
        compiler_params=_cparams("parallel", "arbitrary"),
        name="peer",
    )(h2, u_bf, vt_bf, cnt, e1, rank2, e2, g, b)


def _block_diag2(w):
    z = jnp.zeros_like(w[0])
    return jnp.concatenate([jnp.concatenate([w[0], z], axis=1), jnp.concatenate([z, w[1]], axis=1)], axis=0)


def _layer(h_in, mem2, batch, seq, mem_len, lng, lnb, p):
    n, d = h_in.shape
    (w_in, rwkv_mu, rwkv_w0, rwkv_w2, rwkv_a0, rwkv_a2, rwkv_g2, rwkv_k_k, rwkv_k_a, rwkv_r_k,
     rwkv_gn_g, rwkv_gn_b, gmlp_ln_g, gmlp_ln_b, gmlp_w_s, gmlp_b_s, w_branch, w_mix_out, ln1_g, ln1_b,
     mem_ln_g, mem_ln_b, xattn_w_q, xattn_w_kv, xattn_w_o, ln2_g, ln2_b,
     peer_w_query, peer_sub_keys, peer_u, peer_v, ln3_g, ln3_b, alpha) = p
    width = rwkv_k_k.shape[0]
    rw_cols = rwkv_mu.shape[0]
    gm_cols = 2 * gmlp_ln_g.shape[0]
    chunk = gmlp_w_s.shape[1]
    row1 = lambda t: t.reshape(1, -1)

    tm = TOKEN_TILE
    wa = w_in[:, :rw_cols].astype(BF16)
    wb = w_in[:, rw_cols:rw_cols + gm_cols].astype(BF16)
    wg = w_in[:, rw_cols + gm_cols:].astype(BF16)
    pa, pb, pg = _inproj(h_in, row1(lng), row1(lnb), wa, wb, wg, tm)

    heads_r = width // HEAD_DIM
    bd = jnp.asarray(np.kron(np.eye(heads_r, dtype=np.float32), np.ones((HEAD_DIM, HEAD_DIM), np.float32)), BF16)
    r, v, na, lw, kd, bv, g, bonus = _rwkv_pre(
        pa, seq, row1(rwkv_mu), row1(rwkv_w0), _hilo(_block_diag2(rwkv_w2)), row1(rwkv_a0),
        _hilo(_block_diag2(rwkv_a2)), _hilo(rwkv_g2), row1(rwkv_k_k), row1(rwkv_k_a), row1(rwkv_r_k), bd, tm)
    yf, yb = _scan(r, v, na, lw, kd, bv, batch, seq)

    groups = gmlp_w_s.shape[0]
    wscat = jnp.concatenate([gmlp_w_s[0::2], gmlp_w_s[1::2]], axis=2).astype(BF16)
    bsfull = jnp.repeat(gmlp_b_s.T, HEAD_DIM, axis=1)
    assert groups * HEAD_DIM == width
    h1 = _mixer(h_in, yf, yb, g, bonus, pb, pg, row1(lng), row1(lnb), row1(rwkv_gn_g), row1(rwkv_gn_b), bd,
                row1(gmlp_ln_g), row1(gmlp_ln_b), wscat, bsfull, w_branch.astype(BF16), w_mix_out.astype(BF16),
                row1(ln1_g), row1(ln1_b), alpha, chunk, tm)

    kv = _memkv(mem2, row1(mem_ln_g), row1(mem_ln_b), xattn_w_kv.astype(BF16), mem_len)
    h2 = _xattn(h1, kv, xattn_w_q.astype(BF16), xattn_w_o.astype(BF16), row1(ln2_g), row1(ln2_b),
                alpha, XATTN_HEADS, seq, mem_len, tm)

    pheads = peer_w_query.shape[1] // (2 * LANES)
    cnt, e1, rank2, e2 = _route(h2, peer_w_query, peer_sub_keys, pheads, TOKEN_TILE)
    h3 = _peer(h2, peer_u.astype(BF16), peer_v.astype(BF16).T, cnt, e1, rank2, e2, row1(ln3_g), row1(ln3_b),
               alpha, pheads, TOKEN_TILE, PEER_EXPERT_TILE)
    return h3


def kernel(x, mem, ln_emb_g, ln_emb_b, w_in, rwkv_mu, rwkv_w0, rwkv_w2, rwkv_a0, rwkv_a2, rwkv_g2, rwkv_k_k, rwkv_k_a, rwkv_r_k, rwkv_gn_g, rwkv_gn_b, gmlp_ln_g, gmlp_ln_b, gmlp_w_s, gmlp_b_s, w_branch, w_mix_out, ln1_g, ln1_b, mem_ln_g, mem_ln_b, xattn_w_q, xattn_w_kv, xattn_w_o, ln2_g, ln2_b, peer_w_query, peer_sub_keys, peer_u, peer_v, ln3_g, ln3_b):
    batch, seq, d = x.shape
    mem_len = mem.shape[1]
    depth = w_in.shape[0]
    assert depth == 1, "the layer pipeline applies the embedding LN inside the first layer only"
    alpha = float((2.0 * depth) ** 0.25)
    x2 = x.reshape(batch * seq, d)
    mem2 = mem.reshape(batch * mem_len, d)
    flat = lambda t: t.reshape(-1)
    p = (w_in[0], rwkv_mu[0], flat(rwkv_w0[0]), rwkv_w2[0], flat(rwkv_a0[0]), rwkv_a2[0], rwkv_g2[0],
         rwkv_k_k[0], rwkv_k_a[0], flat(rwkv_r_k[0]), rwkv_gn_g[0], rwkv_gn_b[0], gmlp_ln_g[0], gmlp_ln_b[0],
         gmlp_w_s[0], gmlp_b_s[0], w_branch[0], w_mix_out[0], ln1_g[0], ln1_b[0], mem_ln_g[0], mem_ln_b[0],
         xattn_w_q[0], xattn_w_kv[0], xattn_w_o[0], ln2_g[0], ln2_b[0], peer_w_query[0], peer_sub_keys[0],
         peer_u[0], peer_v[0], ln3_g[0], ln3_b[0], alpha)
    h = _layer(x2, mem2, batch, seq, mem_len, ln_emb_g, ln_emb_b, p)
    return h.reshape(batch, seq, d)
```

```python
import functools

import numpy as np
import jax
import jax.numpy as jnp
from jax import lax
from jax.experimental import pallas as pl
from jax.experimental.pallas import tpu as pltpu

F32 = jnp.float32
BF16 = jnp.bfloat16
HIGHEST = lax.Precision.HIGHEST

LN_EPS = 1e-5
GN_EPS = 64e-5
LANES = 128
HEAD_DIM = 64
SCAN_CHUNK = 64
TOPK = 16
N_KEYS = 128
XATTN_HEADS = 4
TOKEN_TILE = 256
PEER_EXPERT_TILE = 4096
VMEM_LIMIT = 56 * 1024 * 1024


def _cparams(*sem):
    return pltpu.CompilerParams(dimension_semantics=sem, vmem_limit_bytes=VMEM_LIMIT)


def _ln(x, g, b, eps=LN_EPS):
    mu = jnp.mean(x, axis=-1, keepdims=True)
    xc = x - mu
    var = jnp.mean(xc * xc, axis=-1, keepdims=True)
    return xc * lax.rsqrt(var + eps) * g + b


def _mm(a, b):
    return jnp.dot(a.astype(BF16), b.astype(BF16), preferred_element_type=F32)


def _mm_nt(a, b):
    return lax.dot_general(a, b, (((1,), (1,)), ((), ())), preferred_element_type=F32)


def _mm_nt_hi(a, b):
    return lax.dot_general(a, b, (((1,), (1,)), ((), ())), precision=HIGHEST,
                           preferred_element_type=F32)


def _split_bf16(x):
    hi = x.astype(BF16)
    return hi, (x - hi.astype(F32)).astype(BF16)


def _hilo(w):
    hi, lo = _split_bf16(w)
    return jnp.stack([hi, lo])


def _mm_x3(x, w_ref):
    hi, lo = _split_bf16(x)
    w_hi = w_ref[0]
    return (jnp.dot(hi, w_hi, preferred_element_type=F32) + jnp.dot(hi, w_ref[1], preferred_element_type=F32)
            + jnp.dot(lo, w_hi, preferred_element_type=F32))


def _segsum(x, ones):
    hi, lo = _split_bf16(x)
    return jnp.dot(hi, ones, preferred_element_type=F32) + jnp.dot(lo, ones, preferred_element_type=F32)


def _gelu_x2(x):
    return x * (1.0 + lax.erf(x * np.float32(1.0 / np.sqrt(2.0))))


def _gelu(x):
    return 0.5 * _gelu_x2(x)


def _sigmoid(x):
    return 1.0 / (1.0 + jnp.exp(-x))


def _full(shape):
    nd = len(shape)
    return pl.BlockSpec(shape, lambda *_: (0,) * nd)


def _inproj_kernel(x_ref, g_ref, b_ref, wa_ref, wb_ref, wg_ref, pa_ref, pb_ref, pg_ref):
    h = _ln(x_ref[...], g_ref[...], b_ref[...]).astype(BF16)
    pa_ref[...] = jnp.dot(h, wa_ref[...], preferred_element_type=F32)
    pb_ref[...] = jnp.dot(h, wb_ref[...], preferred_element_type=F32)
    pg_ref[...] = jnp.dot(h, wg_ref[...], preferred_element_type=F32)


def _inproj(x2, g, b, wa, wb, wg, tm):
    n, d = x2.shape
    ca, cb, cg = wa.shape[1], wb.shape[1], wg.shape[1]
    row = lambda c: pl.BlockSpec((tm, c), lambda i: (i, 0))
    return pl.pallas_call(
        _inproj_kernel,
        grid=(n // tm,),
        in_specs=[row(d), _full((1, d)), _full((1, d)), _full((d, ca)), _full((d, cb)), _full((d, cg))],
        out_specs=[row(ca), row(cb), row(cg)],
        out_shape=[jax.ShapeDtypeStruct((n, c), F32) for c in (ca, cb, cg)],
        compiler_params=_cparams("parallel"),
        name="inproj",
    )(x2, g, b, wa, wb, wg)


def _rwkv_pre_kernel(tiles_per_seq, width,
                     p_ref, prev_ref, next_ref, mu_ref, w0_ref, w2_ref, a0_ref, a2_ref, g2_ref,
                     kk_ref, ka_ref, rk_ref, bd_ref,
                     r_out, v_out, na_out, lw_out, kd_out, bv_out, g_out, bonus_out):
    i = pl.program_id(0)
    pos = i % tiles_per_seq
    p = p_ref[...]
    tm = p.shape[0]
    row = lax.broadcasted_iota(jnp.int32, (tm, 1), 0)
    prev_row = jnp.where(pos == 0, 0.0, prev_ref[7:8, :])
    next_row = jnp.where(pos == tiles_per_seq - 1, 0.0, next_ref[0:1, :])
    prev = jnp.where(row == 0, prev_row, pltpu.roll(p, 1, 0))
    nxt = jnp.where(row == tm - 1, next_row, pltpu.roll(p, tm - 1, 0))
    ps = p + mu_ref[...] * (0.5 * (prev + nxt) - p)

    w = width
    r = ps[:, 0:w]
    k = ps[:, w:2 * w]
    v = ps[:, 2 * w:3 * w]
    wd = ps[:, 3 * w:3 * w + LANES]
    ad = ps[:, 3 * w + LANES:3 * w + 2 * LANES]
    gd = ps[:, 3 * w + 2 * LANES:3 * w + 3 * LANES]

    wz = w0_ref[...] + _mm_x3(jnp.tanh(wd), w2_ref)
    log_decay = np.float32(-np.exp(-0.5)) * _sigmoid(wz)
    a = _sigmoid(a0_ref[...] + _mm_x3(ad, a2_ref))
    g = _mm_x3(_sigmoid(gd), g2_ref)

    bd = bd_ref[...]
    kk = k * kk_ref[...]
    nrm = jnp.sqrt(_segsum(kk * kk, bd))
    kk = kk / jnp.maximum(nrm, 1e-12)

    ka = ka_ref[...]
    k0 = k * (1.0 + (a[:, 0:w] - 1.0) * ka)
    k1 = k * (1.0 + (a[:, w:2 * w] - 1.0) * ka)
    rk = _segsum(r * (k0 + k1) * rk_ref[...], bd)

    r_out[...] = r
    v_out[...] = v
    na_out[...] = -kk
    lw_out[0] = log_decay[:, 0:w]
    lw_out[1] = log_decay[:, w:2 * w]
    kd_out[0] = k0
    kd_out[1] = k1
    bv_out[0] = kk * a[:, 0:w]
    bv_out[1] = kk * a[:, w:2 * w]
    g_out[...] = g
    bonus_out[...] = rk * v


def _rwkv_pre(pa, seq, mu, w0, w2bd, a0, a2bd, g2, k_k, k_a, r_k, bd, tm):
    n, ca = pa.shape
    w = bd.shape[0]
    tps = seq // tm
    hb = tm // 8
    nb8 = n // 8
    row = lambda c: pl.BlockSpec((tm, c), lambda i: (i, 0))
    row2 = pl.BlockSpec((2, tm, w), lambda i: (0, i, 0))
    outs = [jax.ShapeDtypeStruct((n, w), F32)] * 3 + [jax.ShapeDtypeStruct((2, n, w), F32)] * 3 + \
           [jax.ShapeDtypeStruct((n, w), F32)] * 2
    return pl.pallas_call(
        functools.partial(_rwkv_pre_kernel, tps, w),
        grid=(n // tm,),
        in_specs=[row(ca),
                  pl.BlockSpec((8, ca), lambda i: (jnp.maximum(i * hb - 1, 0), 0)),
                  pl.BlockSpec((8, ca), lambda i: (jnp.minimum((i + 1) * hb, nb8 - 1), 0)),
                  _full((1, ca)), _full((1, 2 * w)), _full(w2bd.shape), _full((1, 2 * w)),
                  _full(a2bd.shape), _full(g2.shape), _full((1, w)), _full((1, w)), _full((1, w)),
                  _full((w, w))],
        out_specs=[row(w), row(w), row(w), row2, row2, row2, row(w), row(w)],
        out_shape=outs,
        compiler_params=_cparams("parallel"),
        name="rwkv_pre",
    )(pa, pa, pa, mu, w0, w2bd, a0, a2bd, g2, k_k, k_a, r_k, bd)


def _scan_kernel(rf_ref, vf_ref, af_ref, rb_ref, vb_ref, ab_ref, lwf_ref, kf_ref, bf_ref, lwb_ref, kb_ref, bb_ref,
                 ms_ref, mi_ref, lc_ref, yf_ref, yb_ref, s_scr):
    c = SCAN_CHUNK
    h2 = 2 * c

    @pl.when(pl.program_id(1) == 0)
    def _():
        s_scr[...] = jnp.zeros_like(s_scr)

    first_head = lax.broadcasted_iota(jnp.int32, (1, LANES), 1) < HEAD_DIM

    def stack(x):
        return jnp.concatenate([jnp.where(first_head, x, 0.0), jnp.where(first_head, 0.0, x)], axis=0)

    dot = functools.partial(jnp.dot, preferred_element_type=F32)
    n_groups = lwf_ref.shape[1] // LANES
    sls = [slice(m * LANES, (m + 1) * LANES) for m in range(n_groups)]

    ar, bk, v_s, bk_e, p_end, strict, incl = [], [], [], [], [], [], []
    dirs = ((rf_ref, vf_ref, af_ref, lwf_ref, kf_ref, bf_ref), (rb_ref, vb_ref, ab_ref, lwb_ref, kb_ref, bb_ref))
    for d, (r_ref, v_ref, a_ref, lw_ref, k_ref, b_ref) in enumerate(dirs):
        lw = lw_ref[...]
        l1 = lw.astype(BF16)
        rest = lw - l1.astype(F32)
        l2 = rest.astype(BF16)
        l3 = (rest - l2.astype(F32)).astype(BF16)
        lc = lc_ref[d]
        cum = dot(lc, l1) + dot(lc, l2) + dot(lc, l3)
        tot = jnp.sum(lw, axis=0, keepdims=True)
        e_neg = jnp.exp(-cum)
        e_end = jnp.exp(tot - cum)
        a_t = a_ref[...] * jnp.exp(cum - lw)
        r_t = r_ref[...] * jnp.exp(cum)
        b_in = b_ref[...]
        k_in = k_ref[...]
        b_t = b_in * e_neg
        k_t = k_in * e_neg
        b_e = b_in * e_end
        k_e = k_in * e_end
        v_in = v_ref[...]
        pe = jnp.exp(tot)
        sd = ms_ref[d] > 0.0
        ic = mi_ref[d] > 0.0
        for s in sls:
            ar.append(jnp.concatenate([stack(a_t[:, s]), stack(r_t[:, s])], axis=0).astype(BF16))
            bk.append(jnp.concatenate([stack(b_t[:, s]), stack(k_t[:, s])], axis=0).astype(BF16))
            bk_e.append(jnp.concatenate([stack(b_e[:, s]), stack(k_e[:, s])], axis=0).astype(BF16))
            v_s.append(stack(v_in[:, s]))
            p_end.append(pe[:, s])
            strict.append(sd)
            incl.append(ic)
    gs = range(2 * n_groups)
    v_b = [x.astype(BF16) for x in v_s]
    state = [s_scr[m] for m in gs]

    l_all = [_mm_nt(ar[m], bk[m]) for m in gs]
    from_state = [_mm_nt(ar[m], state[m].astype(BF16)) for m in gs]
    l_ab = [jnp.where(strict[m], l_all[m][:h2, :h2], 0.0).astype(BF16) for m in gs]
    l_k = [jnp.concatenate([jnp.where(strict[m], l_all[m][:h2, h2:], 0.0),
                            jnp.where(incl[m], l_all[m][h2:, h2:], 0.0)], axis=0).astype(BF16) for m in gs]
    l_rb = [jnp.where(incl[m], l_all[m][h2:, :h2], 0.0).astype(BF16) for m in gs]
    from_v = [dot(l_k[m], v_b[m]) for m in gs]

    x = [from_state[m][:h2] + from_v[m][:h2] for m in gs]
    lp = l_ab
    x = [x[m] + dot(lp[m], x[m].astype(BF16)) for m in gs]
    span = 2
    while span < c:
        lp = [dot(lp[m], lp[m]).astype(BF16) for m in gs]
        x = [x[m] + dot(lp[m], x[m].astype(BF16)) for m in gs]
        span *= 2
    u_s = x

    y_s = [from_state[m][h2:] + from_v[m][h2:] + dot(l_rb[m], u_s[m].astype(BF16)) for m in gs]
    uv_t = [jnp.transpose(jnp.concatenate([u_s[m], v_s[m]], axis=0)).astype(BF16) for m in gs]
    new_state = [state[m] * p_end[m] + dot(uv_t[m], bk_e[m]) for m in gs]
    for m in gs:
        y_ref = yf_ref if m < n_groups else yb_ref
        y_ref[:, sls[m % n_groups]] = y_s[m][:c] + y_s[m][c:]
        s_scr[m] = new_state[m]


def _scan(r, v, na, lw, kd, bv, batch, seq):
    n, w = r.shape
    c = SCAN_CHUNK
    nc = seq // c
    t = np.arange(c)
    before = [(t[None, :] < t[:, None]), (t[None, :] > t[:, None])]
    eye = np.eye(c, dtype=bool)
    tile2 = lambda m: np.tile(m, (2, 2))
    ms = jnp.asarray(np.stack([tile2(m) for m in before]).astype(np.float32))
    mi = jnp.asarray(np.stack([tile2(m | eye) for m in before]).astype(np.float32))
    lc = jnp.asarray(np.stack([(m | eye) for m in before]).astype(np.float32), BF16)

    fwd = pl.BlockSpec((c, w), lambda b, ci: (b * nc + ci, 0))
    bwd = pl.BlockSpec((c, w), lambda b, ci: (b * nc + nc - 1 - ci, 0))
    fwd_d = pl.BlockSpec((None, c, w), lambda b, ci: (0, b * nc + ci, 0))
    bwd_d = pl.BlockSpec((None, c, w), lambda b, ci: (1, b * nc + nc - 1 - ci, 0))
    return pl.pallas_call(
        _scan_kernel,
        grid=(batch, nc),
        in_specs=[fwd, fwd, fwd, bwd, bwd, bwd, fwd_d, fwd_d, fwd_d, bwd_d, bwd_d, bwd_d,
                  _full(ms.shape), _full(mi.shape), _full(lc.shape)],
        out_specs=[fwd, bwd],
        out_shape=[jax.ShapeDtypeStruct((n, w), F32)] * 2,
        scratch_shapes=[pltpu.VMEM((2 * (w // LANES), LANES, LANES), F32)],
        compiler_params=_cparams("parallel", "arbitrary"),
        name="scan",
    )(r, v, na, r, v, na, lw, kd, bv, lw, kd, bv, ms, mi, lc)


def _mixer_kernel(alpha, chunk,
                  x_ref, yf_ref, yb_ref, g_ref, bonus_ref, pb_ref, pg_ref,
                  lng_ref, lnb_ref, gng_ref, gnb_ref, bd_ref, mlg_ref, mlb_ref, ws_ref, bs_ref,
                  wbr_ref, wmix_ref, l1g_ref, l1b_ref, o_ref):
    h0 = _ln(x_ref[...], lng_ref[...], lnb_ref[...])
    bd = bd_ref[...]
    inv = np.float32(1.0 / HEAD_DIM)

    y = yf_ref[...] + yb_ref[...]
    mu = _segsum(y, bd) * inv
    yc = y - mu
    var = _segsum(yc * yc, bd) * inv
    y_a = (yc * lax.rsqrt(var + GN_EPS) * gng_ref[...] + gnb_ref[...] + bonus_ref[...]) * g_ref[...]

    gp = _gelu(pb_ref[...])
    w = gp.shape[1] // 2
    u = gp[:, :w]
    vn = _ln(gp[:, w:], mlg_ref[...], mlb_ref[...])
    tm = u.shape[0]
    first_group = lax.broadcasted_iota(jnp.int32, (1, LANES), 1) < HEAD_DIM
    rows = []
    for ci in range(tm // chunk):
        vc = vn[ci * chunk:(ci + 1) * chunk]
        cols = []
        for m in range(w // LANES):
            vp = vc[:, m * LANES:(m + 1) * LANES]
            stacked = jnp.concatenate([jnp.where(first_group, vp, 0.0), jnp.where(first_group, 0.0, vp)], axis=0)
            cols.append(_mm(ws_ref[m], stacked))
        rows.append(jnp.concatenate(cols, axis=1) + bs_ref[...])
    sv = jnp.concatenate(rows, axis=0)
    y_b = u * sv

    pg = pg_ref[...]
    d = pg.shape[1] // 2
    merged = _sigmoid(pg[:, :d]) * _mm(y_a, wbr_ref[0]) + _sigmoid(pg[:, d:]) * _mm(y_b, wbr_ref[1])
    mix = _mm(merged, wmix_ref[...])
    o_ref[...] = _ln(alpha * h0 + mix, l1g_ref[...], l1b_ref[...])


def _mixer(x2, yf, yb, g, bonus, pb, pg, lng, lnb, gng, gnb, bd, mlg, mlb, wscat, bsfull, wbr, wmix, l1g, l1b,
           alpha, chunk, tm):
    n, d = x2.shape
    w = g.shape[1]
    row = lambda c: pl.BlockSpec((tm, c), lambda i: (i, 0))
    return pl.pallas_call(
        functools.partial(_mixer_kernel, alpha, chunk),
        grid=(n // tm,),
        in_specs=[row(d), row(w), row(w), row(w), row(w), row(2 * w), row(2 * d),
                  _full((1, d)), _full((1, d)), _full((1, w)), _full((1, w)), _full((w, w)),
                  _full((1, w)), _full((1, w)), _full(wscat.shape), _full(bsfull.shape),
                  _full(wbr.shape), _full(wmix.shape), _full((1, d)), _full((1, d))],
        out_specs=row(d),
        out_shape=jax.ShapeDtypeStruct((n, d), F32),
        compiler_params=_cparams("parallel"),
        name="mixer",
    )(x2, yf, yb, g, bonus, pb, pg, lng, lnb, gng, gnb, bd, mlg, mlb, wscat, bsfull, wbr, wmix, l1g, l1b)


def _memkv_kernel(m_ref, g_ref, b_ref, w_ref, o_ref):
    o_ref[...] = _mm(_ln(m_ref[...], g_ref[...], b_ref[...]), w_ref[...]).astype(BF16)


def _memkv(mem2, g, b, wkv, tm):
    n, d = mem2.shape
    return pl.pallas_call(
        _memkv_kernel,
        grid=(n // tm,),
        in_specs=[pl.BlockSpec((tm, d), lambda i: (i, 0)), _full((1, d)), _full((1, d)), _full(wkv.shape)],
        out_specs=pl.BlockSpec((tm, wkv.shape[1]), lambda i: (i, 0)),
        out_shape=jax.ShapeDtypeStruct((n, wkv.shape[1]), BF16),
        compiler_params=_cparams("parallel"),
        name="memkv",
    )(mem2, g, b, wkv)


def _xattn_kernel(alpha, heads, h_ref, kv_ref, wq_ref, wo_ref, g_ref, b_ref, o_ref):
    h = h_ref[...]
    d = h.shape[1]
    hd = d // heads
    q = _mm(h, wq_ref[...])
    kv = kv_ref[...]
    scale = np.float32(hd ** -0.5)
    hs = range(heads)
    qb = q.astype(BF16)
    s = [_mm_nt(qb[:, i * hd:(i + 1) * hd], kv[:, i * hd:(i + 1) * hd]) * scale for i in hs]
    s = [x - jnp.max(x, axis=-1, keepdims=True) for x in s]
    e = [jnp.exp(x) for x in s]
    prob = [x / jnp.sum(x, axis=-1, keepdims=True) for x in e]
    o = [_mm(prob[i], kv[:, d + i * hd:d + (i + 1) * hd]) for i in hs]
    out = _mm(jnp.concatenate(o, axis=1), wo_ref[...])
    o_ref[...] = _ln(alpha * h + out, g_ref[...], b_ref[...])


def _xattn(h1, kv, wq, wo, g, b, alpha, heads, seq, mem_len, tm):
    n, d = h1.shape
    tps = seq // tm
    return pl.pallas_call(
        functools.partial(_xattn_kernel, alpha, heads),
        grid=(n // tm,),
        in_specs=[pl.BlockSpec((tm, d), lambda i: (i, 0)),
                  pl.BlockSpec((mem_len, 2 * d), lambda i: (i // tps, 0)),
                  _full(wq.shape), _full(wo.shape), _full((1, d)), _full((1, d))],
        out_specs=pl.BlockSpec((tm, d), lambda i: (i, 0)),
        out_shape=jax.ShapeDtypeStruct((n, d), F32),
        compiler_params=_cparams("parallel"),
        name="xattn",
    )(h1, kv, wq, wo, g, b)


def _extract_topk(s, pos, k):
    rank = jnp.full(s.shape, float(k), F32)
    vals = []
    for j in range(k):
        m = jnp.max(s, axis=0, keepdims=True)
        first = jnp.min(jnp.where(s == m, pos, np.float32(np.inf)), axis=0, keepdims=True)
        sel = pos == first
        rank = jnp.where(sel, float(j), rank)
        s = jnp.where(sel, -jnp.inf, s)
        vals.append(m)
    return vals, rank


def _candidate_layout(k):
    rest = ((2, 2), (3, 2), (4, 2), (2, 3), (3, 3), (2, 4), (15, 15), (15, 15))
    pieces = [("row", 0, 0, 16), ("row", 1, 0, 8), ("col", 0, 0, 16), ("col", 1, 0, 8), ("pairs", rest, 0, 8)]
    pos, seen = [], set()
    for kind, fixed, start, n in pieces:
        for o in range(start, start + n):
            i, j = fixed[o] if kind == "pairs" else (fixed, o) if kind == "row" else (o, fixed)
            ok = (i + 1) * (j + 1) <= k and (i, j) not in seen
            if ok:
                seen.add((i, j))
            pos.append(float(i * k + j) if ok else -1.0)
    assert len(seen) == sum(k // (i + 1) for i in range(k))
    return pieces, np.asarray(pos, np.float32)


def _sort16_comparators():
    def merge(lo, hi, r):
        step = r * 2
        if step < hi - lo:
            yield from merge(lo, hi, step)
            yield from merge(lo + r, hi, step)
            yield from ((i, i + r) for i in range(lo + r, hi - r, step))
        else:
            yield (lo, lo + r)

    def sort(lo, hi):
        if hi - lo >= 1:
            mid = lo + (hi - lo) // 2
            yield from sort(lo, mid)
            yield from sort(mid + 1, hi)
            yield from merge(lo, hi, 1)

    return tuple(sort(0, 15))


def _top16_sorted(s):
    def exchange(v, i, j):
        v[i], v[j] = jnp.maximum(v[i], v[j]), jnp.minimum(v[i], v[j])

    v = [s[8 * r:8 * r + 8] for r in range(16)]
    for i, j in _sort16_comparators():
        exchange(v, i, j)
    for shift in (4, 2, 1):
        w = [pltpu.roll(x, shift, 0) for x in v]
        v = [jnp.maximum(v[i], w[15 - i]) for i in range(16)]
        for stride in (8, 4, 2, 1):
            for i in range(16):
                if not i & stride:
                    exchange(v, i, i + stride)
    return v


def _route_kernel(heads, h_ref, wq_ref, sk_ref, cpos_ref, cnt_ref, e1_ref, rank2_ref, e2_ref, q_scr):
    h = h_ref[...]
    tt = h.shape[0]
    q_scr[...] = _mm(h, wq_ref[...])
    k = TOPK
    pieces, _ = _candidate_layout(k)
    cpos = cpos_ref[...]
    cvalid = cpos >= 0.0
    cpos_inf = jnp.where(cvalid, cpos, np.float32(np.inf))
    key_pos = lax.broadcasted_iota(jnp.int32, (N_KEYS, tt), 0).astype(F32)
    row16 = lax.broadcasted_iota(jnp.int32, (k, 1), 0)

    def scores(hd):
        off = pl.multiple_of(hd * (2 * LANES), 2 * LANES)
        s1 = _mm_nt_hi(sk_ref[0], q_scr[:, pl.ds(off, LANES)])
        s2 = _mm_nt_hi(sk_ref[1], q_scr[:, pl.ds(off + LANES, LANES)])
        return s1, s2

    def pair_selection(top1, top2):
        t1 = jnp.concatenate(top1, axis=0)
        t2 = jnp.concatenate(top2, axis=0)
        parts = []
        for kind, fixed, start, n in pieces:
            if kind == "row":
                parts.append(top1[fixed] + t2[start:start + n])
            elif kind == "col":
                parts.append(t1[start:start + n] + top2[fixed])
            else:
                parts.append(jnp.concatenate([top1[i] + top2[j] for i, j in fixed], axis=0))
        cand = jnp.where(cvalid, jnp.concatenate(parts, axis=0), -jnp.inf)
        best, crank = _extract_topk(cand, cpos_inf, k)
        chosen = jnp.where(crank < float(k), 1.0, 0.0)
        n16 = jnp.zeros((k, tt), F32)
        r0 = 0
        for kind, fixed, start, n in pieces:
            c = chosen[r0:r0 + n]
            r0 += n
            if kind == "row":
                n16 = n16 + jnp.where(row16 == fixed, jnp.sum(c, axis=0, keepdims=True), 0.0)
            elif kind == "pairs":
                for o, (i, _) in enumerate(fixed):
                    n16 = n16 + jnp.where(row16 == i, c[o:o + 1], 0.0)
            elif n == k:
                n16 = n16 + c
            else:
                n16 = n16 + jnp.concatenate([c, jnp.zeros((k - n, tt), F32)], axis=0)
        return best, n16

    def emit(hd, s1, s2, max1, max2, best, cnt_dense, rank2):
        z = jnp.zeros_like(best[0])
        for j in range(k):
            z = z + jnp.exp(best[j] - best[0])
        cnt_ref[hd] = cnt_dense
        e1_ref[hd] = jnp.exp(s1 - max1) / (2.0 * z)
        rank2_ref[hd] = rank2.astype(BF16)
        e2_ref[hd] = jnp.exp(s2 - max2).astype(BF16)

    def head_by_sorting(hd, bad):
        s1, s2 = scores(hd)
        v1 = _top16_sorted(s1)
        v2 = _top16_sorted(s2)
        top1 = [x[0:1] for x in v1]
        top2 = [x[0:1] for x in v2]
        best, n16 = pair_selection(top1, top2)
        rank2 = jnp.full_like(s2, float(k))
        cnt_dense = jnp.zeros_like(s1)
        for i in reversed(range(k)):
            rank2 = jnp.where(s2 >= top2[i], float(i), rank2)
            cnt_dense = jnp.where(s1 == top1[i], n16[i:i + 1], cnt_dense)
        in1 = jnp.sum(jnp.where(s1 >= top1[k - 1], 1.0, 0.0), axis=0, keepdims=True)
        in2 = jnp.sum(jnp.where(rank2 < float(k), 1.0, 0.0), axis=0, keepdims=True)
        ties = jnp.where(in1 == float(k), 0.0, 1.0) + jnp.where(in2 == float(k), 0.0, 1.0)
        for i in range(k - 1):
            ties = ties + jnp.where(top1[i] > top1[i + 1], 0.0, 1.0) + jnp.where(top2[i] > top2[i + 1], 0.0, 1.0)
        emit(hd, s1, s2, top1[0], top2[0], best, cnt_dense, rank2)
        return bad + ties

    def head_exact(hd, carry):
        s1, s2 = scores(hd)
        top1, rank1 = _extract_topk(s1, key_pos, k)
        top2, rank2 = _extract_topk(s2, key_pos, k)
        best, n16 = pair_selection(top1, top2)
        cnt_dense = jnp.zeros_like(s1)
        for i in range(k):
            cnt_dense = cnt_dense + jnp.where(rank1 == float(i), n16[i:i + 1], 0.0)
        emit(hd, s1, s2, top1[0], top2[0], best, cnt_dense, rank2)
        return carry

    bad = lax.fori_loop(0, heads, head_by_sorting, jnp.zeros((1, tt), F32))

    @pl.when(jnp.sum(bad) > 0.0)
    def _():
        lax.fori_loop(0, heads, head_exact, 0)


def _route(h2, wq, sk, heads, tt):
    n, d = h2.shape
    _, pos = _candidate_layout(TOPK)
    cpos = jnp.asarray(np.broadcast_to(pos[:, None], (pos.shape[0], tt)).copy())
    blk = pl.BlockSpec((heads, N_KEYS, tt), lambda i: (0, 0, i))
    return pl.pallas_call(
        functools.partial(_route_kernel, heads),
        grid=(n // tt,),
        in_specs=[pl.BlockSpec((tt, d), lambda i: (i, 0)), _full(wq.shape), _full(sk.shape), _full(cpos.shape)],
        out_specs=[blk] * 4,
        out_shape=[jax.ShapeDtypeStruct((heads, N_KEYS, n), F32)] * 2
                  + [jax.ShapeDtypeStruct((heads, N_KEYS, n), BF16)] * 2,
        scratch_shapes=[pltpu.VMEM((tt, wq.shape[1]), F32)],
        compiler_params=_cparams("parallel"),
        name="route",
    )(h2, wq.astype(BF16), sk, cpos)


PEER_CHUNK = 256


def _peer_kernel(alpha, heads, h_ref, u_ref, vt_ref, cnt_ref, e1_ref, rank2_ref, e2_ref, g_ref, b_ref, o_ref,
                 acc_scr, hbt_scr):
    e = pl.program_id(1)
    te = u_ref.shape[0]
    tt = h_ref.shape[0]
    sub = 16
    grp = N_KEYS // sub
    per_chunk = PEER_CHUNK // N_KEYS
    dot = functools.partial(jnp.dot, preferred_element_type=F32)

    @pl.when(e == 0)
    def _():
        acc_scr[...] = jnp.zeros_like(acc_scr)
        hbt_scr[...] = jnp.transpose(h_ref[...]).astype(BF16)

    for c in range(te // PEER_CHUNK):
        rows = slice(c * PEER_CHUNK, (c + 1) * PEER_CHUNK)
        act = dot(u_ref[rows, :], hbt_scr[...])
        ws = []
        for bl in range(per_chunk):
            il = c * per_chunk + bl
            r = il % 8
            base = pl.multiple_of(e * (te // N_KEYS) + (il - r), 8)
            gate = jnp.zeros((grp, sub, tt), BF16)
            for hd in range(heads):
                cnt = jnp.broadcast_to(cnt_ref[hd, pl.ds(base, 8), :][r:r + 1], (sub, tt)).astype(BF16)
                e1 = jnp.broadcast_to(e1_ref[hd, pl.ds(base, 8), :][r:r + 1], (sub, tt)).astype(BF16)
                rank2 = rank2_ref[hd].reshape(grp, sub, tt)
                e2 = e2_ref[hd].reshape(grp, sub, tt)
                gate = gate + jnp.where(rank2 < cnt[None], e2, jnp.zeros_like(e2)) * e1[None]
            a = act[bl * N_KEYS:(bl + 1) * N_KEYS]
            ws.append((gate * _gelu_x2(a).astype(BF16).reshape(grp, sub, tt)).reshape(N_KEYS, tt))
        acc_scr[...] += dot(vt_ref[:, rows], jnp.concatenate(ws, axis=0))

    @pl.when(e == pl.num_programs(1) - 1)
    def _():
        y = jnp.transpose(acc_scr[...])
        o_ref[...] = _ln(alpha * h_ref[...] + y, g_ref[...], b_ref[...])


def _peer(h2, u_bf, vt_bf, cnt, e1, rank2, e2, g, b, alpha, heads, tt, te):
    n, d = h2.shape
    assert te % PEER_CHUNK == 0
    blk = pl.BlockSpec((heads, N_KEYS, tt), lambda i, e: (0, 0, i))
    return pl.pallas_call(
        functools.partial(_peer_kernel, alpha, heads),
        grid=(n // tt, u_bf.shape[0] // te),
        in_specs=[pl.BlockSpec((tt, d), lambda i, e: (i, 0)),
                  pl.BlockSpec((te, d), lambda i, e: (e, 0)),
                  pl.BlockSpec((d, te), lambda i, e: (0, e)),
                  blk, blk, blk, blk, _full((1, d)), _full((1, d))],
        out_specs=pl.BlockSpec((tt, d), lambda i, e: (i, 0)),
        out_shape=jax.ShapeDtypeStruct((n, d), F32),
        scratch_shapes=[pltpu.VMEM((d, tt), F32), pltpu.VMEM((d, tt), BF16)],
        compiler_params=_cparams("parallel", "arbitrary"),
        name="peer",
    )(h2, u_bf, vt_bf, cnt, e1, rank2, e2, g, b)


def _block_diag2(w):
    z = jnp.zeros_like(w[0])
    return jnp.concatenate([jnp.concatenate([w[0], z], axis=1), jnp.concatenate([z, w[1]], axis=1)], axis=0)


def _layer(h_in, mem2, batch, seq, mem_len, lng, lnb, p):
    n, d = h_in.shape
    (w_in, rwkv_mu, rwkv_w0, rwkv_w2, rwkv_a0, rwkv_a2, rwkv_g2, rwkv_k_k, rwkv_k_a, rwkv_r_k,
     rwkv_gn_g, rwkv_gn_b, gmlp_ln_g, gmlp_ln_b, gmlp_w_s, gmlp_b_s, w_branch, w_mix_out, ln1_g, ln1_b,
     mem_ln_g, mem_ln_b, xattn_w_q, xattn_w_kv, xattn_w_o, ln2_g, ln2_b,
     peer_w_query, peer_sub_keys, peer_u, peer_v, ln3_g, ln3_b, alpha) = p
    width = rwkv_k_k.shape[0]
    rw_cols = rwkv_mu.shape[0]
    gm_cols = 2 * gmlp_ln_g.shape[0]
    chunk = gmlp_w_s.shape[1]
    row1 = lambda t: t.reshape(1, -1)

    tm = TOKEN_TILE
    wa = w_in[:, :rw_cols].astype(BF16)
    wb = w_in[:, rw_cols:rw_cols + gm_cols].astype(BF16)
    wg = w_in[:, rw_cols + gm_cols:].astype(BF16)
    pa, pb, pg = _inproj(h_in, row1(lng), row1(lnb), wa, wb, wg, tm)

    heads_r = width // HEAD_DIM
    bd = jnp.asarray(np.kron(np.eye(heads_r, dtype=np.float32), np.ones((HEAD_DIM, HEAD_DIM), np.float32)), BF16)
    r, v, na, lw, kd, bv, g, bonus = _rwkv_pre(
        pa, seq, row1(rwkv_mu), row1(rwkv_w0), _hilo(_block_diag2(rwkv_w2)), row1(rwkv_a0),
        _hilo(_block_diag2(rwkv_a2)), _hilo(rwkv_g2), row1(rwkv_k_k), row1(rwkv_k_a), row1(rwkv_r_k), bd, tm)
    yf, yb = _scan(r, v, na, lw, kd, bv, batch, seq)

    groups = gmlp_w_s.shape[0]
    wscat = jnp.concatenate([gmlp_w_s[0::2], gmlp_w_s[1::2]], axis=2).astype(BF16)
    bsfull = jnp.repeat(gmlp_b_s.T, HEAD_DIM, axis=1)
    assert groups * HEAD_DIM == width
    h1 = _mixer(h_in, yf, yb, g, bonus, pb, pg, row1(lng), row1(lnb), row1(rwkv_gn_g), row1(rwkv_gn_b), bd,
                row1(gmlp_ln_g), row1(gmlp_ln_b), wscat, bsfull, w_branch.astype(BF16), w_mix_out.astype(BF16),
                row1(ln1_g), row1(ln1_b), alpha, chunk, tm)

    kv = _memkv(mem2, row1(mem_ln_g), row1(mem_ln_b), xattn_w_kv.astype(BF16), mem_len)
    h2 = _xattn(h1, kv, xattn_w_q.astype(BF16), xattn_w_o.astype(BF16), row1(ln2_g), row1(ln2_b),
                alpha, XATTN_HEADS, seq, mem_len, tm)

    pheads = peer_w_query.shape[1] // (2 * LANES)
    cnt, e1, rank2, e2 = _route(h2, peer_w_query, peer_sub_keys, pheads, TOKEN_TILE)
    h3 = _peer(h2, peer_u.astype(BF16), peer_v.astype(BF16).T, cnt, e1, rank2, e2, row1(ln3_g), row1(ln3_b),
               alpha, pheads, TOKEN_TILE, PEER_EXPERT_TILE)
    return h3


def kernel(x, mem, ln_emb_g, ln_emb_b, w_in, rwkv_mu, rwkv_w0, rwkv_w2, rwkv_a0, rwkv_a2, rwkv_g2, rwkv_k_k, rwkv_k_a, rwkv_r_k, rwkv_gn_g, rwkv_gn_b, gmlp_ln_g, gmlp_ln_b, gmlp_w_s, gmlp_b_s, w_branch, w_mix_out, ln1_g, ln1_b, mem_ln_g, mem_ln_b, xattn_w_q, xattn_w_kv, xattn_w_o, ln2_g, ln2_b, peer_w_query, peer_sub_keys, peer_u, peer_v, ln3_g, ln3_b):
    batch, seq, d = x.shape
    mem_len = mem.shape[1]
    depth = w_in.shape[0]
    assert depth == 1, "the layer pipeline applies the embedding LN inside the first layer only"
    alpha = float((2.0 * depth) ** 0.25)
    x2 = x.reshape(batch * seq, d)
    mem2 = mem.reshape(batch * mem_len, d)
    flat = lambda t: t.reshape(-1)
    p = (w_in[0], rwkv_mu[0], flat(rwkv_w0[0]), rwkv_w2[0], flat(rwkv_a0[0]), rwkv_a2[0], rwkv_g2[0],
         rwkv_k_k[0], rwkv_k_a[0], flat(rwkv_r_k[0]), rwkv_gn_g[0], rwkv_gn_b[0], gmlp_ln_g[0], gmlp_ln_b[0],
         gmlp_w_s[0], gmlp_b_s[0], w_branch[0], w_mix_out[0], ln1_g[0], ln1_b[0], mem_ln_g[0], mem_ln_b[0],
         xattn_w_q[0], xattn_w_kv[0], xattn_w_o[0], ln2_g[0], ln2_b[0], peer_w_query[0], peer_sub_keys[0],
         peer_u[0], peer_v[0], ln3_g[0], ln3_b[0], alpha)
    h = _layer(x2, mem2, batch, seq, mem_len, ln_emb_g, ln_emb_b, p)
    return h.reshape(batch, seq, d)
```
